```python
import math
import jax, jax.numpy as jnp
from jax import lax
import numpy as np

D_MODEL = 2048
BATCH = 2
SEQ = 4096
DEPTH = 2

N_META = 16
POOL_WINDOWS = (2, 4, 8, 16)
POOL_GROUPS = len(POOL_WINDOWS)
POOL_GROUP_DIM = D_MODEL // 16
POOL_WIDTH = POOL_GROUPS * POOL_GROUP_DIM
HGRN_WIDTH = 3 * D_MODEL // 8
HGRN_HEAD_DIM = 128
HGRN_HEADS = HGRN_WIDTH // HGRN_HEAD_DIM
CHUNK = 16
DIFF_WIDTH = 3 * D_MODEL // 8
DIFF_HEAD_DIM = 64
DIFF_HEADS = DIFF_WIDTH // (2 * DIFF_HEAD_DIM)
Q_BLOCK = 128
MASK_VALUE = -1e30
D_FF = 11 * D_MODEL // 4
GATE_WIDTH = 3 * D_MODEL
IN_COLS = POOL_WIDTH + 4 * HGRN_WIDTH + 3 * DIFF_WIDTH + GATE_WIDTH
DEEPNORM_ALPHA = (2.0 * DEPTH) ** 0.25
DEEPNORM_BETA = (8.0 * DEPTH) ** -0.25
LN_EPS = 1e-5
RMS_EPS = 1e-6

kernel_name = 'hybrid_pool_hgrn2_diffattn_macaron_deepnorm'


def layer_norm(x, g, b):
    xf = x.astype(jnp.float32)
    mu = jnp.mean(xf, axis=-1, keepdims=True)
    var = jnp.mean(jnp.square(xf - mu), axis=-1, keepdims=True)
    return ((xf - mu) * lax.rsqrt(var + LN_EPS) * g + b).astype(x.dtype)


def rms_norm(x, g):
    xf = x.astype(jnp.float32)
    return xf * lax.rsqrt(jnp.mean(jnp.square(xf), axis=-1, keepdims=True) + RMS_EPS) * g


def swiglu_ffn(h, w_in, w_out):
    a, u = jnp.split(h @ w_in, 2, axis=-1)
    return (jax.nn.silu(a) * u) @ w_out


def pool_mixer(u, w_group, scale):
    B, L, _ = u.shape
    uf = u.astype(jnp.float32)
    cs = jnp.pad(jnp.cumsum(uf, axis=1), ((0, 0), (1, 0), (0, 0)))
    pos = jnp.arange(L)
    outs = []
    for gi, w in enumerate(POOL_WINDOWS):
        sl = slice(gi * POOL_GROUP_DIM, (gi + 1) * POOL_GROUP_DIM)
        c = cs[:, :, sl]
        start = jnp.maximum(pos + 1 - w, 0)
        win_sum = c[:, 1:] - c[:, start]
        cnt = (pos + 1 - start).astype(jnp.float32)
        outs.append(win_sum / cnt[None, :, None] - uf[:, :, sl])
    p = jnp.stack(outs, axis=2)
    y = jnp.einsum('blgc,gcd->blgd', p, w_group.astype(jnp.float32)).reshape(B, L, POOL_WIDTH)
    return (y * scale).astype(u.dtype)


def hgrn2_mixer(q, fz, v, og, lb, norm_g):
    B, L, _ = q.shape
    H, K = HGRN_HEADS, HGRN_HEAD_DIM
    f32 = jnp.float32
    lb = lb.astype(f32)
    f = lb + (1.0 - lb) * jax.nn.sigmoid(fz.astype(f32))
    logf = jnp.log(f)
    k = 1.0 - f
    qs = q.astype(f32) * (K ** -0.5)
    nC = L // CHUNK

    def to_chunks(t):
        return t.astype(f32).reshape(B, nC, CHUNK, H, K).transpose(1, 0, 3, 2, 4)

    tri = jnp.tril(jnp.ones((CHUNK, CHUNK), dtype=bool))[:, :, None]

    def step(S, xs):
        qc, kc, vc, gc = xs
        b = jnp.cumsum(gc, axis=2)
        diff = b[:, :, :, None, :] - b[:, :, None, :, :]
        decay = jnp.where(tri, jnp.exp(jnp.where(tri, diff, 0.0)), 0.0)
        attn = jnp.einsum('bhtk,bhsk,bhtsk->bhts', qc, kc, decay)
        o = (jnp.einsum('bhts,bhsv->bhtv', attn, vc)
             + jnp.einsum('bhtk,bhkv->bhtv', qc * jnp.exp(b), S))
        b_last = b[:, :, -1]
        S = (jnp.exp(b_last)[..., None] * S
             + jnp.einsum('bhsk,bhsv->bhkv', kc * jnp.exp(b_last[:, :, None] - b), vc))
        return S, o

    S0 = jnp.zeros((B, H, K, K), f32)
    _, o = lax.scan(step, S0, (to_chunks(qs), to_chunks(k), to_chunks(v), to_chunks(logf)))
    o = o.transpose(1, 0, 3, 2, 4).reshape(B, L, H, K)
    o = rms_norm(o, norm_g) * jax.nn.silu(og.astype(f32).reshape(B, L, H, K))
    return o.reshape(B, L, HGRN_WIDTH).astype(q.dtype)


def diff_attention(q, k, v, lam, norm_g, lambda_init):
    B, L, _ = q.shape
    H, d = DIFF_HEADS, DIFF_HEAD_DIM
    scale = d ** -0.5
    q = q.reshape(B, L, H, 2, d)
    k = k.reshape(B, L, H, 2, d)
    vt = v.reshape(B, L, H, 2 * d).transpose(0, 2, 1, 3)
    nQ = -(-L // Q_BLOCK)
    Lp = nQ * Q_BLOCK
    qp = jnp.pad(q, ((0, 0), (0, Lp - L), (0, 0), (0, 0), (0, 0)))
    qb = qp.reshape(B, nQ, Q_BLOCK, H, 2, d).transpose(1, 0, 3, 4, 2, 5)
    kt = k.transpose(0, 2, 3, 1, 4)
    kpos = jnp.arange(L)

    def block(args):
        qblk, start = args
        s = jnp.einsum('bhmqd,bhmkd->bhmqk', qblk, kt).astype(jnp.float32) * scale
        qpos = start + jnp.arange(Q_BLOCK)
        mask = kpos[None, :] <= qpos[:, None]
        p = jax.nn.softmax(jnp.where(mask, s, MASK_VALUE), axis=-1)
        a = p[:, :, 0] - lam * p[:, :, 1]
        return jnp.einsum('bhqk,bhkv->bhqv', a.astype(vt.dtype), vt)

    o = lax.map(block, (qb, jnp.arange(nQ) * Q_BLOCK))
    o = o.transpose(1, 0, 3, 2, 4).reshape(B, Lp, H, 2 * d)[:, :L]
    o = rms_norm(o, norm_g) * (1.0 - lambda_init)
    return o.reshape(B, L, H * 2 * d).astype(v.dtype)


def setup_inputs(seed: int = 0) -> dict:
    key = jax.random.key(seed)
    ks = jax.random.split(key, 17)
    f32 = jnp.float32

    def nrm(k, shape, s):
        return jax.random.normal(k, shape, f32) * s

    return {
        'x': nrm(ks[0], (BATCH, SEQ, D_MODEL), 1.0),
        'meta_tokens': nrm(ks[1], (N_META, D_MODEL), 1.0),
        'ln_g': 1.0 + nrm(ks[2], (DEPTH, 3, D_MODEL), 0.02),
        'ln_b': nrm(ks[3], (DEPTH, 3, D_MODEL), 0.02),
        'ffn_w_in': nrm(ks[4], (DEPTH, 2, D_MODEL, 2 * D_FF), D_MODEL ** -0.5),
        'ffn_w_out': nrm(ks[5], (DEPTH, 2, D_FF, D_MODEL), D_FF ** -0.5 * DEEPNORM_BETA),
        'w_in': nrm(ks[6], (DEPTH, D_MODEL, IN_COLS), D_MODEL ** -0.5),
        'pool_w': nrm(ks[7], (DEPTH, POOL_GROUPS, POOL_GROUP_DIM, POOL_GROUP_DIM), POOL_GROUP_DIM ** -0.5),
        'pool_scale': 1.0 + nrm(ks[8], (DEPTH, POOL_WIDTH), 0.1),
        'hgrn_lower_bounds': nrm(ks[9], (DEPTH, HGRN_WIDTH), 0.1),
        'hgrn_norm_g': 1.0 + nrm(ks[10], (DEPTH, HGRN_HEAD_DIM), 0.02),
        'diff_lambda': nrm(ks[11], (DEPTH, 4, DIFF_HEAD_DIM), 0.1),
        'diff_norm_g': 1.0 + nrm(ks[12], (DEPTH, 2 * DIFF_HEAD_DIM), 0.02),
        'w_branch_pool': nrm(ks[13], (DEPTH, POOL_WIDTH, D_MODEL), POOL_WIDTH ** -0.5),
        'w_branch_hgrn': nrm(ks[14], (DEPTH, HGRN_WIDTH, D_MODEL), HGRN_WIDTH ** -0.5),
        'w_branch_diff': nrm(ks[15], (DEPTH, DIFF_WIDTH, D_MODEL), DIFF_WIDTH ** -0.5),
        'w_out': nrm(ks[16], (DEPTH, D_MODEL, D_MODEL), D_MODEL ** -0.5 * DEEPNORM_BETA),
    }


def reference(x, meta_tokens, ln_g, ln_b, ffn_w_in, ffn_w_out, w_in, pool_w, pool_scale,
              hgrn_lower_bounds, hgrn_norm_g, diff_lambda, diff_norm_g,
              w_branch_pool, w_branch_hgrn, w_branch_diff, w_out):
    B = x.shape[0]
    meta = jnp.broadcast_to(meta_tokens[None].astype(x.dtype), (B, N_META, D_MODEL))
    h = jnp.concatenate([meta, x], axis=1)

    lb_sm = jax.nn.softmax(hgrn_lower_bounds.astype(jnp.float32), axis=0)
    lower_bounds = jnp.cumsum(lb_sm, axis=0) - lb_sm[0]

    o_pool = POOL_WIDTH
    o_hgrn = o_pool + 4 * HGRN_WIDTH
    o_diff = o_hgrn + 3 * DIFF_WIDTH

    for l in range(DEPTH):
        h = layer_norm(DEEPNORM_ALPHA * h + 0.5 * swiglu_ffn(h, ffn_w_in[l, 0], ffn_w_out[l, 0]),
                       ln_g[l, 0], ln_b[l, 0])

        z = h @ w_in[l]
        z_pool = z[..., :o_pool]
        hq, hf, hv, hg = jnp.split(z[..., o_pool:o_hgrn], 4, axis=-1)
        dq, dk, dv = jnp.split(z[..., o_hgrn:o_diff], 3, axis=-1)
        gates = jax.nn.sigmoid(z[..., o_diff:].astype(jnp.float32))
        g_pool, g_hgrn, g_diff = jnp.split(gates, 3, axis=-1)

        y_pool = pool_mixer(z_pool, pool_w[l], pool_scale[l])
        y_hgrn = hgrn2_mixer(hq, hf, hv, hg, lower_bounds[l], hgrn_norm_g[l])
        lambda_init = 0.8 - 0.6 * math.exp(-0.3 * l)
        lam_p = diff_lambda[l].astype(jnp.float32)
        lam = (jnp.exp(jnp.sum(lam_p[0] * lam_p[1])) - jnp.exp(jnp.sum(lam_p[2] * lam_p[3]))
               + lambda_init)
        y_diff = diff_attention(dq, dk, dv, lam, diff_norm_g[l], lambda_init)

        merged = (g_pool * (y_pool @ w_branch_pool[l])
                  + g_hgrn * (y_hgrn @ w_branch_hgrn[l])
                  + g_diff * (y_diff @ w_branch_diff[l])).astype(h.dtype)
        h = layer_norm(DEEPNORM_ALPHA * h + merged @ w_out[l], ln_g[l, 1], ln_b[l, 1])

        h = layer_norm(DEEPNORM_ALPHA * h + 0.5 * swiglu_ffn(h, ffn_w_in[l, 1], ffn_w_out[l, 1]),
                       ln_g[l, 2], ln_b[l, 2])

    return h[:, N_META:]
```

```python
import functools
import math

import numpy as np
import jax
import jax.numpy as jnp
from jax import lax
from jax.experimental import pallas as pl
from jax.experimental.pallas import tpu as pltpu

F32 = jnp.float32
BF16 = jnp.bfloat16

D_MODEL = 2048
BATCH = 2
SEQ = 4096
DEPTH = 2
N_META = 16
SEQ_TOTAL = N_META + SEQ
LANES = 128
LP = -(-SEQ_TOTAL // LANES) * LANES
NTOK = BATCH * LP

POOL_WINDOWS = (2, 4, 8, 16)
POOL_GROUP_DIM = 128
POOL_WIDTH = 512
POOL_HALO = 16
HGRN_WIDTH = 768
HGRN_HEAD_DIM = 128
HGRN_HEADS = 6
HGRN_CHUNK = 128
DIFF_WIDTH = 768
DIFF_HEAD_DIM = 64
DIFF_HEADS = 6
D_FF = 5632
DEEPNORM_ALPHA = (2.0 * DEPTH) ** 0.25
LN_EPS = 1e-5
RMS_EPS = 1e-6

O_POOL = POOL_WIDTH
O_HGRN = O_POOL + 4 * HGRN_WIDTH
O_DIFF = O_HGRN + 3 * DIFF_WIDTH

TM = 528
TM_FFN = 528
TF = 512
TN = 512
TQ = 384
TP = 384
VMEM_LIMIT = 56 * 1024 * 1024


def _sigmoid(x):
    return 1.0 / (1.0 + jnp.exp(-x))


def _layer_norm(y, g, b):
    mu = jnp.mean(y, axis=-1, keepdims=True)
    yc = y - mu
    var = jnp.mean(yc * yc, axis=-1, keepdims=True)
    return yc * lax.rsqrt(var + LN_EPS) * g + b


def _dot(a, b):
    return jnp.dot(a, b, preferred_element_type=F32)


def _dot_nt(a, b):
    return lax.dot_general(a, b, (((1,), (1,)), ((), ())), preferred_element_type=F32)


def _dot_tn(a, b):
    return lax.dot_general(a, b, (((0,), (0,)), ((), ())), preferred_element_type=F32)


def _ffn_kernel(h_ref, wa_ref, wu_ref, wo_ref, g_ref, b_ref, o_ref, ob_ref, hb_ref, acc_ref):
    j = pl.program_id(1)

    @pl.when(j == 0)
    def _():
        hb_ref[...] = h_ref[...].astype(BF16)
        acc_ref[...] = jnp.zeros_like(acc_ref)

    hb = hb_ref[...]
    a = _dot(hb, wa_ref[...])
    u = _dot(hb, wu_ref[...])
    act = (a * _sigmoid(a)) * u
    acc_ref[...] += _dot(act.astype(BF16), wo_ref[...])

    @pl.when(j == pl.num_programs(1) - 1)
    def _():
        y = DEEPNORM_ALPHA * h_ref[...] + 0.5 * acc_ref[...]
        out = _layer_norm(y, g_ref[...], b_ref[...])
        o_ref[...] = out
        ob_ref[...] = out.astype(BF16)


def _ffn(h, w_in_b, w_out_b, g, b):
    nff = D_FF // TF
    return pl.pallas_call(
        _ffn_kernel,
        grid=(NTOK // TM_FFN, nff),
        in_specs=[
            pl.BlockSpec((TM_FFN, D_MODEL), lambda i, j: (i, 0)),
            pl.BlockSpec((D_MODEL, TF), lambda i, j: (0, j)),
            pl.BlockSpec((D_MODEL, TF), lambda i, j: (0, j + nff)),
            pl.BlockSpec((TF, D_MODEL), lambda i, j: (j, 0)),
            pl.BlockSpec((1, D_MODEL), lambda i, j: (0, 0)),
            pl.BlockSpec((1, D_MODEL), lambda i, j: (0, 0)),
        ],
        out_specs=[
            pl.BlockSpec((TM_FFN, D_MODEL), lambda i, j: (i, 0)),
            pl.BlockSpec((TM_FFN, D_MODEL), lambda i, j: (i, 0)),
        ],
        out_shape=[
            jax.ShapeDtypeStruct((NTOK, D_MODEL), F32),
            jax.ShapeDtypeStruct((NTOK, D_MODEL), BF16),
        ],
        scratch_shapes=[
            pltpu.VMEM((TM_FFN, D_MODEL), BF16),
            pltpu.VMEM((TM_FFN, D_MODEL), F32),
        ],
        compiler_params=pltpu.CompilerParams(
            dimension_semantics=("parallel", "arbitrary"), vmem_limit_bytes=VMEM_LIMIT),
        name="ffn",
    )(h, w_in_b, w_in_b, w_out_b, g, b)


def _proj_kernel(x_ref, w_ref, o_ref, *, gate):
    z = _dot(x_ref[...], w_ref[...])
    if gate:
        z = _sigmoid(z)
    o_ref[...] = z.astype(o_ref.dtype)


def _proj(xb, wb, out_dtype, gate, name):
    ncols = wb.shape[1]
    tn = TN if ncols % TN == 0 else 256
    return pl.pallas_call(
        functools.partial(_proj_kernel, gate=gate),
        grid=(NTOK // TM, ncols // tn),
        in_specs=[
            pl.BlockSpec((TM, D_MODEL), lambda i, j: (i, 0)),
            pl.BlockSpec((D_MODEL, tn), lambda i, j: (0, j)),
        ],
        out_specs=pl.BlockSpec((TM, tn), lambda i, j: (i, j)),
        out_shape=jax.ShapeDtypeStruct((NTOK, ncols), out_dtype),
        compiler_params=pltpu.CompilerParams(
            dimension_semantics=("parallel", "arbitrary"), vmem_limit_bytes=VMEM_LIMIT),
        name=name,
    )(xb, wb)


def _pool_kernel(z_ref, w_ref, s_ref, o_ref, ext_ref):
    i = pl.program_id(1)
    u = z_ref[0]

    @pl.when(i == 0)
    def _():
        ext_ref[0:POOL_HALO, :] = jnp.zeros((POOL_HALO, POOL_WIDTH), F32)

    ext_ref[POOL_HALO:, :] = u
    pos = i * TP + lax.broadcasted_iota(jnp.int32, (TP, 1), 0)
    outs = []
    for gi, w in enumerate(POOL_WINDOWS):
        sl = slice(gi * POOL_GROUP_DIM, (gi + 1) * POOL_GROUP_DIM)
        win = ext_ref[POOL_HALO:POOL_HALO + TP, sl]
        for d in range(1, w):
            win = win + ext_ref[POOL_HALO - d:POOL_HALO - d + TP, sl]
        cnt = jnp.minimum(pos + 1, w).astype(F32)
        p = win / cnt - u[:, sl]
        y = _dot(p.astype(BF16), w_ref[gi])
        outs.append(y * s_ref[:, sl])
    o_ref[0] = jnp.concatenate(outs, axis=-1).astype(o_ref.dtype)
    ext_ref[0:POOL_HALO, :] = u[TP - POOL_HALO:, :]


def _pool(z_pool, pool_w_b, pool_scale):
    return pl.pallas_call(
        _pool_kernel,
        grid=(BATCH, LP // TP),
        in_specs=[
            pl.BlockSpec((1, TP, POOL_WIDTH), lambda b, i: (b, i, 0)),
            pl.BlockSpec((4, POOL_GROUP_DIM, POOL_GROUP_DIM), lambda b, i: (0, 0, 0)),
            pl.BlockSpec((1, POOL_WIDTH), lambda b, i: (0, 0)),
        ],
        out_specs=pl.BlockSpec((1, TP, POOL_WIDTH), lambda b, i: (b, i, 0)),
        out_shape=jax.ShapeDtypeStruct((BATCH, LP, POOL_WIDTH), BF16),
        scratch_shapes=[pltpu.VMEM((POOL_HALO + TP, POOL_WIDTH), F32)],
        compiler_params=pltpu.CompilerParams(
            dimension_semantics=("parallel", "arbitrary"), vmem_limit_bytes=VMEM_LIMIT),
        name="pool",
    )(z_pool, pool_w_b, pool_scale)


def _pair_levels():
    t = np.arange(HGRN_CHUNK)[:, None]
    s = np.arange(HGRN_CHUNK)[None, :]
    x = t ^ s
    lvl = np.where(x > 0, 1 << np.floor(np.log2(np.maximum(x, 1))).astype(np.int64), 0)
    return np.where(t >= s, lvl, -1).astype(np.int32)


def _hgrn_kernel(q_ref, f_ref, v_ref, g_ref, lb_ref, ng_ref, lvl_ref, o_ref, st_ref, *, layer):
    c = pl.program_id(1)
    T, K = HGRN_CHUNK, HGRN_HEAD_DIM

    @pl.when(c == 0)
    def _():
        st_ref[...] = jnp.zeros_like(st_ref)

    lbp = lb_ref[...]
    e = jnp.exp(lbp - jnp.max(lbp, axis=0, keepdims=True))
    sm = e / jnp.sum(e, axis=0, keepdims=True)
    lb_all = jnp.sum(sm[0:layer + 1], axis=0, keepdims=True) - sm[0:1]

    lvl = lvl_ref[...]
    row = lax.broadcasted_iota(jnp.int32, (T, K), 0)
    ng = ng_ref[...]

    for h in range(HGRN_HEADS):
        hs = slice(h * K, (h + 1) * K)
        lb = lb_all[:, hs]
        f = lb + (1.0 - lb) * _sigmoid(f_ref[0, :, hs])
        logf = jnp.log(f)
        kk = 1.0 - f
        q = q_ref[0, :, hs] * (K ** -0.5)
        v = v_ref[0, :, hs].astype(BF16)

        b = logf
        for sh in (1, 2, 4, 8, 16, 32, 64):
            b = b + jnp.where(row >= sh, pltpu.roll(b, sh, axis=0), 0.0)

        st = st_ref[h]
        o = _dot_nt((q * jnp.exp(b)).astype(BF16), st.astype(BF16))

        att = jnp.where(lvl == 0, _dot_nt(q.astype(BF16), kk.astype(BF16)), 0.0)
        e_end = b
        p_beg = jnp.where(row >= 1, pltpu.roll(b, 1, axis=0), 0.0)
        for w in (1, 2, 4, 8, 16, 32, 64):
            lower = (row & w) == 0
            arg = jnp.minimum(jnp.where(lower, e_end - b, b - p_beg), 0.0)
            dec = jnp.exp(arg)
            sc = _dot_nt((q * dec).astype(BF16), (kk * dec).astype(BF16))
            att = att + jnp.where(lvl == w, sc, 0.0)
            if w < T // 2:
                e_end = jnp.where(lower, pltpu.roll(e_end, T - w, axis=0), e_end)
                p_beg = jnp.where(lower, p_beg, pltpu.roll(p_beg, w, axis=0))
        o = o + _dot(att.astype(BF16), v)

        b_last = b[T - 1:T, :]
        khat = (kk * jnp.exp(b_last - b)).astype(BF16)
        st_ref[h] = jnp.exp(b_last) * st + _dot_tn(v, khat)

        ms = jnp.mean(o * o, axis=-1, keepdims=True)
        og = g_ref[0, :, hs]
        y = o * lax.rsqrt(ms + RMS_EPS) * ng * (og * _sigmoid(og))
        o_ref[0, :, hs] = y.astype(o_ref.dtype)


def _hgrn(z_hgrn, lower_bounds, norm_g, layer):
    nchunk = LP // HGRN_CHUNK
    blk = lambda k: pl.BlockSpec((1, HGRN_CHUNK, HGRN_WIDTH), lambda b, c, k=k: (b, c, k))
    return pl.pallas_call(
        functools.partial(_hgrn_kernel, layer=layer),
        grid=(BATCH, nchunk),
        in_specs=[
            blk(0), blk(1), blk(2), blk(3),
            pl.BlockSpec((DEPTH, HGRN_WIDTH), lambda b, c: (0, 0)),
            pl.BlockSpec((1, HGRN_HEAD_DIM), lambda b, c: (0, 0)),
            pl.BlockSpec((HGRN_CHUNK, HGRN_CHUNK), lambda b, c: (0, 0)),
        ],
        out_specs=pl.BlockSpec((1, HGRN_CHUNK, HGRN_WIDTH), lambda b, c: (b, c, 0)),
        out_shape=jax.ShapeDtypeStruct((BATCH, LP, HGRN_WIDTH), BF16),
        scratch_shapes=[pltpu.VMEM((HGRN_HEADS, HGRN_HEAD_DIM, HGRN_HEAD_DIM), F32)],
        compiler_params=pltpu.CompilerParams(
            dimension_semantics=("parallel", "arbitrary"), vmem_limit_bytes=VMEM_LIMIT),
        name="hgrn",
    )(z_hgrn, z_hgrn, z_hgrn, z_hgrn, lower_bounds, norm_g, jnp.asarray(_pair_levels()))


def _attn_kernel(q_ref, k_ref, v_ref, lam_ref, ng_ref, o_ref, *, lambda_init):
    i = pl.program_id(2)
    d = DIFF_HEAD_DIM
    q = q_ref[0] * (d ** -0.5)
    lane = lax.broadcasted_iota(jnp.int32, (TQ, 2 * d), 1)
    zero = jnp.zeros_like(q)
    q2 = jnp.concatenate([jnp.where(lane < d, q, zero), jnp.where(lane >= d, q, zero)], axis=0)

    def step(j, carry, masked):
        m, l, acc = carry
        off = pl.multiple_of(j * TQ, TQ)
        k = k_ref[0, pl.ds(off, TQ), :]
        v = v_ref[0, pl.ds(off, TQ), :]
        s = _dot_nt(q2, k)
        if masked:
            r = lax.broadcasted_iota(jnp.int32, (2 * TQ, TQ), 0)
            cidx = lax.broadcasted_iota(jnp.int32, (2 * TQ, TQ), 1)
            r = jnp.where(r >= TQ, r - TQ, r)
            s = jnp.where(cidx <= r, s, -1e30)
        m_new = jnp.maximum(m, jnp.max(s, axis=-1, keepdims=True))
        p = jnp.exp(s - m_new)
        alpha = jnp.exp(m - m_new)
        l = alpha * l + jnp.sum(p, axis=-1, keepdims=True)
        acc = alpha * acc + _dot(p.astype(BF16), v)
        return m_new, l, acc

    init = (jnp.full((2 * TQ, 1), -1e30, F32), jnp.zeros((2 * TQ, 1), F32),
            jnp.zeros((2 * TQ, 2 * d), F32))
    carry = lax.fori_loop(0, i, functools.partial(step, masked=False), init)
    m, l, acc = step(i, carry, masked=True)

    lp = lam_ref[...]
    lam = (jnp.exp(jnp.sum(lp[0:1] * lp[1:2], axis=-1, keepdims=True))
           - jnp.exp(jnp.sum(lp[2:3] * lp[3:4], axis=-1, keepdims=True)) + lambda_init)
    on = acc / l
    o = on[0:TQ] - lam * on[TQ:]
    ms = jnp.mean(o * o, axis=-1, keepdims=True)
    y = o * lax.rsqrt(ms + RMS_EPS) * ng_ref[...] * (1.0 - lambda_init)
    o_ref[0] = y.astype(o_ref.dtype)


def _attn(z_diff, lam_p, norm_g, lambda_init):
    H = DIFF_HEADS
    return pl.pallas_call(
        functools.partial(_attn_kernel, lambda_init=lambda_init),
        grid=(BATCH, H, LP // TQ),
        in_specs=[
            pl.BlockSpec((1, TQ, 2 * DIFF_HEAD_DIM), lambda b, h, i: (b, i, h)),
            pl.BlockSpec((1, LP, 2 * DIFF_HEAD_DIM), lambda b, h, i: (b, 0, H + h)),
            pl.BlockSpec((1, LP, 2 * DIFF_HEAD_DIM), lambda b, h, i: (b, 0, 2 * H + h)),
            pl.BlockSpec((4, DIFF_HEAD_DIM), lambda b, h, i: (0, 0)),
            pl.BlockSpec((1, 2 * DIFF_HEAD_DIM), lambda b, h, i: (0, 0)),
        ],
        out_specs=pl.BlockSpec((1, TQ, 2 * DIFF_HEAD_DIM), lambda b, h, i: (b, i, h)),
        out_shape=jax.ShapeDtypeStruct((BATCH, LP, DIFF_WIDTH), BF16),
        compiler_params=pltpu.CompilerParams(
            dimension_semantics=("parallel", "parallel", "arbitrary"),
            vmem_limit_bytes=VMEM_LIMIT),
        name="diff_attn",
    )(z_diff, z_diff, z_diff, lam_p, norm_g)


def _merge_kernel(h_ref, yp_ref, yh_ref, yd_ref, gp_ref, gh_ref, gd_ref,
                  wp_ref, wh_ref, wd_ref, wo_ref, g_ref, b_ref, o_ref, acc_ref):
    j = pl.program_id(1)

    @pl.when(j == 0)
    def _():
        acc_ref[...] = jnp.zeros_like(acc_ref)

    merged = (gp_ref[...] * _dot(yp_ref[...], wp_ref[...])
              + gh_ref[...] * _dot(yh_ref[...], wh_ref[...])
              + gd_ref[...] * _dot(yd_ref[...], wd_ref[...]))
    acc_ref[...] += _dot(merged.astype(BF16), wo_ref[...])

    @pl.when(j == pl.num_programs(1) - 1)
    def _():
        y = DEEPNORM_ALPHA * h_ref[...] + acc_ref[...]
        o_ref[...] = _layer_norm(y, g_ref[...], b_ref[...])


def _merge(h, y_pool, y_hgrn, y_diff, gates, wp, wh, wd, wo, g, b):
    nj = D_MODEL // TN
    row = lambda w: pl.BlockSpec((TM, w), lambda i, j: (i, 0))
    gate = lambda k: pl.BlockSpec((TM, TN), lambda i, j, k=k: (i, k * nj + j))
    colw = lambda w: pl.BlockSpec((w, TN), lambda i, j: (0, j))
    vec = pl.BlockSpec((1, D_MODEL), lambda i, j: (0, 0))
    return pl.pallas_call(
        _merge_kernel,
        grid=(NTOK // TM, nj),
        in_specs=[
            row(D_MODEL), row(POOL_WIDTH), row(HGRN_WIDTH), row(DIFF_WIDTH),
            gate(0), gate(1), gate(2),
            colw(POOL_WIDTH), colw(HGRN_WIDTH), colw(DIFF_WIDTH),
            pl.BlockSpec((TN, D_MODEL), lambda i, j: (j, 0)),
            vec, vec,
        ],
        out_specs=pl.BlockSpec((TM, D_MODEL), lambda i, j: (i, 0)),
        out_shape=jax.ShapeDtypeStruct((NTOK, D_MODEL), F32),
        scratch_shapes=[pltpu.VMEM((TM, D_MODEL), F32)],
        compiler_params=pltpu.CompilerParams(
            dimension_semantics=("parallel", "arbitrary"), vmem_limit_bytes=VMEM_LIMIT),
        name="merge",
    )(h, y_pool, y_hgrn, y_diff, gates, gates, gates, wp, wh, wd, wo, g, b)


def kernel(x, meta_tokens, ln_g, ln_b, ffn_w_in, ffn_w_out, w_in, pool_w, pool_scale,
           hgrn_lower_bounds, hgrn_norm_g, diff_lambda, diff_norm_g,
           w_branch_pool, w_branch_hgrn, w_branch_diff, w_out):
    assert x.shape == (BATCH, SEQ, D_MODEL)
    meta = jnp.broadcast_to(meta_tokens[None].astype(x.dtype), (BATCH, N_META, D_MODEL))
    pad = jnp.zeros((BATCH, LP - SEQ_TOTAL, D_MODEL), x.dtype)
    h = jnp.concatenate([meta, x, pad], axis=1).reshape(NTOK, D_MODEL)

    for l in range(DEPTH):
        vec = lambda a, k: a[l, k][None, :]
        h, hb = _ffn(h, ffn_w_in[l, 0].astype(BF16), ffn_w_out[l, 0].astype(BF16),
                     vec(ln_g, 0), vec(ln_b, 0))

        wl = w_in[l].astype(BF16)
        z_pool = _proj(hb, wl[:, :O_POOL], F32, False, "proj_pool")
        z_hgrn = _proj(hb, wl[:, O_POOL:O_HGRN], F32, False, "proj_hgrn")
        z_diff = _proj(hb, wl[:, O_HGRN:O_DIFF], BF16, False, "proj_diff")
        gates = _proj(hb, wl[:, O_DIFF:], F32, True, "proj_gates")

        y_pool = _pool(z_pool.reshape(BATCH, LP, POOL_WIDTH), pool_w[l].astype(BF16),
                       pool_scale[l][None, :])
        y_hgrn = _hgrn(z_hgrn.reshape(BATCH, LP, 4 * HGRN_WIDTH), hgrn_lower_bounds,
                       hgrn_norm_g[l][None, :], l)
        lambda_init = 0.8 - 0.6 * math.exp(-0.3 * l)
        y_diff = _attn(z_diff.reshape(BATCH, LP, 3 * DIFF_WIDTH), diff_lambda[l],
                       diff_norm_g[l][None, :], lambda_init)

        h = _merge(h, y_pool.reshape(NTOK, POOL_WIDTH), y_hgrn.reshape(NTOK, HGRN_WIDTH),
                   y_diff.reshape(NTOK, DIFF_WIDTH), gates,
                   w_branch_pool[l].astype(BF16), w_branch_hgrn[l].astype(BF16),
                   w_branch_diff[l].astype(BF16), w_out[l].astype(BF16),
                   vec(ln_g, 1), vec(ln_b, 1))

        h, _ = _ffn(h, ffn_w_in[l, 1].astype(BF16), ffn_w_out[l, 1].astype(BF16),
                    vec(ln_g, 2), vec(ln_b, 2))

    return h.reshape(BATCH, LP, D_MODEL)[:, N_META:SEQ_TOTAL]
```

```python
import functools
import math

import numpy as np
import jax
import jax.numpy as jnp
from jax import lax
from jax.experimental import pallas as pl
from jax.experimental.pallas import tpu as pltpu

F32 = jnp.float32
BF16 = jnp.bfloat16

D_MODEL = 2048
BATCH = 2
SEQ = 4096
DEPTH = 2
N_META = 16
SEQ_TOTAL = N_META + SEQ
LANES = 128
LP = -(-SEQ_TOTAL // LANES) * LANES
NTOK = BATCH * LP

POOL_WINDOWS = (2, 4, 8, 16)
POOL_GROUP_DIM = 128
POOL_WIDTH = 512
POOL_HALO = 16
HGRN_WIDTH = 768
HGRN_HEAD_DIM = 128
HGRN_HEADS = 6
HGRN_CHUNK = 128
DIFF_WIDTH = 768
DIFF_HEAD_DIM = 64
DIFF_HEADS = 6
D_FF = 5632
DEEPNORM_ALPHA = (2.0 * DEPTH) ** 0.25
LN_EPS = 1e-5
RMS_EPS = 1e-6

O_POOL = POOL_WIDTH
O_HGRN = O_POOL + 4 * HGRN_WIDTH
O_DIFF = O_HGRN + 3 * DIFF_WIDTH

TM = 528
TM_FFN = 528
TM_PROJ = 1056
TF = 512
TN_MERGE = 256
TQ = 384
TP = 384
VMEM_LIMIT = 56 * 1024 * 1024


def _sigmoid(x):
    return 1.0 / (1.0 + jnp.exp(-x))


def _layer_norm(y, g, b):
    mu = jnp.mean(y, axis=-1, keepdims=True)
    yc = y - mu
    var = jnp.mean(yc * yc, axis=-1, keepdims=True)
    return yc * lax.rsqrt(var + LN_EPS) * g + b


def _dot(a, b):
    return jnp.dot(a, b, preferred_element_type=F32)


def _dot_nt(a, b):
    return lax.dot_general(a, b, (((1,), (1,)), ((), ())), preferred_element_type=F32)


def _dot_tn(a, b):
    return lax.dot_general(a, b, (((0,), (0,)), ((), ())), preferred_element_type=F32)


def _ffn_kernel(h_ref, wa_ref, wu_ref, wo_ref, g_ref, b_ref, o_ref, ob_ref, hb_ref, acc_ref):
    j = pl.program_id(1)

    @pl.when(j == 0)
    def _():
        hb_ref[...] = h_ref[...].astype(BF16)
        acc_ref[...] = jnp.zeros_like(acc_ref)

    hb = hb_ref[...]
    a = _dot(hb, wa_ref[...])
    u = _dot(hb, wu_ref[...])
    act = (a * _sigmoid(a)) * u
    acc_ref[...] += _dot(act.astype(BF16), wo_ref[...])

    @pl.when(j == pl.num_programs(1) - 1)
    def _():
        y = DEEPNORM_ALPHA * h_ref[...] + 0.5 * acc_ref[...]
        out = _layer_norm(y, g_ref[...], b_ref[...])
        o_ref[...] = out
        ob_ref[...] = out.astype(BF16)


def _ffn(h, w_in_b, w_out_b, g, b):
    nff = D_FF // TF
    return pl.pallas_call(
        _ffn_kernel,
        grid=(NTOK // TM_FFN, nff),
        in_specs=[
            pl.BlockSpec((TM_FFN, D_MODEL), lambda i, j: (i, 0)),
            pl.BlockSpec((D_MODEL, TF), lambda i, j: (0, j)),
            pl.BlockSpec((D_MODEL, TF), lambda i, j: (0, j + nff)),
            pl.BlockSpec((TF, D_MODEL), lambda i, j: (j, 0)),
            pl.BlockSpec((1, D_MODEL), lambda i, j: (0, 0)),
            pl.BlockSpec((1, D_MODEL), lambda i, j: (0, 0)),
        ],
        out_specs=[
            pl.BlockSpec((TM_FFN, D_MODEL), lambda i, j: (i, 0)),
            pl.BlockSpec((TM_FFN, D_MODEL), lambda i, j: (i, 0)),
        ],
        out_shape=[
            jax.ShapeDtypeStruct((NTOK, D_MODEL), F32),
            jax.ShapeDtypeStruct((NTOK, D_MODEL), BF16),
        ],
        scratch_shapes=[
            pltpu.VMEM((TM_FFN, D_MODEL), BF16),
            pltpu.VMEM((TM_FFN, D_MODEL), F32),
        ],
        compiler_params=pltpu.CompilerParams(
            dimension_semantics=("parallel", "arbitrary"), vmem_limit_bytes=VMEM_LIMIT),
        name="ffn",
    )(h, w_in_b, w_in_b, w_out_b, g, b)


def _proj_kernel(x_ref, w_ref, o_ref):
    o_ref[...] = _dot(x_ref[...], w_ref[...]).astype(o_ref.dtype)


def _proj(xb, wb, col0, ncols, tn, out_dtype, name):
    first = col0 // tn
    return pl.pallas_call(
        _proj_kernel,
        grid=(NTOK // TM_PROJ, ncols // tn),
        in_specs=[
            pl.BlockSpec((TM_PROJ, D_MODEL), lambda i, j: (i, 0)),
            pl.BlockSpec((D_MODEL, tn), lambda i, j: (0, first + j)),
        ],
        out_specs=pl.BlockSpec((TM_PROJ, tn), lambda i, j: (i, j)),
        out_shape=jax.ShapeDtypeStruct((NTOK, ncols), out_dtype),
        compiler_params=pltpu.CompilerParams(
            dimension_semantics=("parallel", "arbitrary"), vmem_limit_bytes=VMEM_LIMIT),
        name=name,
    )(xb, wb)


def _pool_kernel(z_ref, w_ref, s_ref, o_ref, ext_ref):
    i = pl.program_id(1)
    u = z_ref[0]

    @pl.when(i == 0)
    def _():
        ext_ref[0:POOL_HALO, :] = jnp.zeros((POOL_HALO, POOL_WIDTH), F32)

    ext_ref[POOL_HALO:, :] = u
    pos = i * TP + lax.broadcasted_iota(jnp.int32, (TP, 1), 0)
    outs = []
    for gi, w in enumerate(POOL_WINDOWS):
        sl = slice(gi * POOL_GROUP_DIM, (gi + 1) * POOL_GROUP_DIM)
        win = ext_ref[POOL_HALO:POOL_HALO + TP, sl]
        for d in range(1, w):
            win = win + ext_ref[POOL_HALO - d:POOL_HALO - d + TP, sl]
        cnt = jnp.minimum(pos + 1, w).astype(F32)
        p = win / cnt - u[:, sl]
        y = _dot(p.astype(BF16), w_ref[gi])
        outs.append(y * s_ref[:, sl])
    o_ref[0] = jnp.concatenate(outs, axis=-1).astype(o_ref.dtype)
    ext_ref[0:POOL_HALO, :] = u[TP - POOL_HALO:, :]


def _pool(z_pool, pool_w_b, pool_scale):
    return pl.pallas_call(
        _pool_kernel,
        grid=(BATCH, LP // TP),
        in_specs=[
            pl.BlockSpec((1, TP, POOL_WIDTH), lambda b, i: (b, i, 0)),
            pl.BlockSpec((4, POOL_GROUP_DIM, POOL_GROUP_DIM), lambda b, i: (0, 0, 0)),
            pl.BlockSpec((1, POOL_WIDTH), lambda b, i: (0, 0)),
        ],
        out_specs=pl.BlockSpec((1, TP, POOL_WIDTH), lambda b, i: (b, i, 0)),
        out_shape=jax.ShapeDtypeStruct((BATCH, LP, POOL_WIDTH), BF16),
        scratch_shapes=[pltpu.VMEM((POOL_HALO + TP, POOL_WIDTH), F32)],
        compiler_params=pltpu.CompilerParams(
            dimension_semantics=("parallel", "arbitrary"), vmem_limit_bytes=VMEM_LIMIT),
        name="pool",
    )(z_pool, pool_w_b, pool_scale)


def _pair_levels():
    t = np.arange(HGRN_CHUNK)[:, None]
    s = np.arange(HGRN_CHUNK)[None, :]
    x = t ^ s
    lvl = np.where(x > 0, 1 << np.floor(np.log2(np.maximum(x, 1))).astype(np.int64), 0)
    return np.where(t >= s, lvl, -1).astype(np.int32)


def _hgrn_kernel(q_ref, f_ref, v_ref, g_ref, lb_ref, ng_ref, lvl_ref, o_ref, st_ref, *, layer):
    c = pl.program_id(1)
    T, K = HGRN_CHUNK, HGRN_HEAD_DIM

    @pl.when(c == 0)
    def _():
        st_ref[...] = jnp.zeros_like(st_ref)

    lbp = lb_ref[...]
    e = jnp.exp(lbp - jnp.max(lbp, axis=0, keepdims=True))
    sm = e / jnp.sum(e, axis=0, keepdims=True)
    lb_all = jnp.sum(sm[0:layer + 1], axis=0, keepdims=True) - sm[0:1]

    lvl = lvl_ref[...]
    row = lax.broadcasted_iota(jnp.int32, (T, K), 0)
    ng = ng_ref[...]

    for h in range(HGRN_HEADS):
        hs = slice(h * K, (h + 1) * K)
        lb = lb_all[:, hs]
        f = lb + (1.0 - lb) * _sigmoid(f_ref[0, :, hs])
        logf = jnp.log(f)
        kk = 1.0 - f
        q = q_ref[0, :, hs] * (K ** -0.5)
        v = v_ref[0, :, hs].astype(BF16)

        b = logf
        for sh in (1, 2, 4, 8, 16, 32, 64):
            b = b + jnp.where(row >= sh, pltpu.roll(b, sh, axis=0), 0.0)

        st = st_ref[h]
        o = _dot_nt((q * jnp.exp(b)).astype(BF16), st.astype(BF16))

        att = jnp.where(lvl == 0, _dot_nt(q.astype(BF16), kk.astype(BF16)), 0.0)
        e_end = b
        p_beg = jnp.where(row >= 1, pltpu.roll(b, 1, axis=0), 0.0)
        for w in (1, 2, 4, 8, 16, 32, 64):
            lower = (row & w) == 0
            arg = jnp.minimum(jnp.where(lower, e_end - b, b - p_beg), 0.0)
            dec = jnp.exp(arg)
            sc = _dot_nt((q * dec).astype(BF16), (kk * dec).astype(BF16))
            att = att + jnp.where(lvl == w, sc, 0.0)
            if w < T // 2:
                e_end = jnp.where(lower, pltpu.roll(e_end, T - w, axis=0), e_end)
                p_beg = jnp.where(lower, p_beg, pltpu.roll(p_beg, w, axis=0))
        o = o + _dot(att.astype(BF16), v)

        b_last = b[T - 1:T, :]
        khat = (kk * jnp.exp(b_last - b)).astype(BF16)
        st_ref[h] = jnp.exp(b_last) * st + _dot_tn(v, khat)

        ms = jnp.mean(o * o, axis=-1, keepdims=True)
        og = g_ref[0, :, hs]
        y = o * lax.rsqrt(ms + RMS_EPS) * ng * (og * _sigmoid(og))
        o_ref[0, :, hs] = y.astype(o_ref.dtype)


def _hgrn(z_hgrn, lower_bounds, norm_g, layer):
    nchunk = LP // HGRN_CHUNK
    blk = lambda k: pl.BlockSpec((1, HGRN_CHUNK, HGRN_WIDTH), lambda b, c, k=k: (b, c, k))
    return pl.pallas_call(
        functools.partial(_hgrn_kernel, layer=layer),
        grid=(BATCH, nchunk),
        in_specs=[
            blk(0), blk(1), blk(2), blk(3),
            pl.BlockSpec((DEPTH, HGRN_WIDTH), lambda b, c: (0, 0)),
            pl.BlockSpec((1, HGRN_HEAD_DIM), lambda b, c: (0, 0)),
            pl.BlockSpec((HGRN_CHUNK, HGRN_CHUNK), lambda b, c: (0, 0)),
        ],
        out_specs=pl.BlockSpec((1, HGRN_CHUNK, HGRN_WIDTH), lambda b, c: (b, c, 0)),
        out_shape=jax.ShapeDtypeStruct((BATCH, LP, HGRN_WIDTH), BF16),
        scratch_shapes=[pltpu.VMEM((HGRN_HEADS, HGRN_HEAD_DIM, HGRN_HEAD_DIM), F32)],
        compiler_params=pltpu.CompilerParams(
            dimension_semantics=("parallel", "arbitrary"), vmem_limit_bytes=VMEM_LIMIT),
        name="hgrn",
    )(z_hgrn, z_hgrn, z_hgrn, z_hgrn, lower_bounds, norm_g, jnp.asarray(_pair_levels()))


def _attn_kernel(q_ref, k_ref, v_ref, lam_ref, ng_ref, o_ref,
                 sa_ref, sb_ref, m_ref, l_ref, acc_ref, *, lambda_init):
    i = pl.program_id(2)
    d = DIFF_HEAD_DIM
    q = q_ref[0] * (d ** -0.5)
    lane = lax.broadcasted_iota(jnp.int32, (TQ, 2 * d), 1)
    zero = jnp.zeros_like(q)
    q2 = jnp.concatenate([jnp.where(lane < d, q, zero), jnp.where(lane >= d, q, zero)], axis=0)

    def scores(j, dst_ref):
        off = pl.multiple_of(j * TQ, TQ)
        dst_ref[...] = _dot_nt(k_ref[0, pl.ds(off, TQ), :], q2)

    def update(j, src_ref, masked):
        off = pl.multiple_of(j * TQ, TQ)
        v = v_ref[0, pl.ds(off, TQ), :]
        s = src_ref[...]
        if masked:
            kr = lax.broadcasted_iota(jnp.int32, (TQ, 2 * TQ), 0)
            qc = lax.broadcasted_iota(jnp.int32, (TQ, 2 * TQ), 1)
            qc = jnp.where(qc >= TQ, qc - TQ, qc)
            s = jnp.where(kr <= qc, s, -1e30)
        m = m_ref[...]
        m_new = jnp.maximum(m, jnp.max(s, axis=0, keepdims=True))
        p = jnp.exp(s - m_new)
        alpha = jnp.exp(m - m_new)
        m_ref[...] = m_new
        l_ref[...] = alpha * l_ref[...] + jnp.sum(p, axis=0, keepdims=True)
        acc_ref[...] = alpha * acc_ref[...] + _dot_tn(v, p.astype(BF16))

    m_ref[...] = jnp.full(m_ref.shape, -1e30, F32)
    l_ref[...] = jnp.zeros(l_ref.shape, F32)
    acc_ref[...] = jnp.zeros(acc_ref.shape, F32)

    scores(0, sa_ref)
    scores(i, sb_ref)
    update(i, sb_ref, masked=True)

    @pl.when(i % 2 == 1)
    def _():
        scores(i - 1, sb_ref)
        update(i - 1, sb_ref, masked=False)

    def pair(t, carry):
        a = 2 * t
        scores(a + 1, sb_ref)
        update(a, sa_ref, masked=False)
        scores(a + 2, sa_ref)
        update(a + 1, sb_ref, masked=False)
        return carry

    lax.fori_loop(0, i // 2, pair, 0)

    lp = lam_ref[...]
    lam = (jnp.exp(jnp.sum(lp[0:1] * lp[1:2], axis=-1, keepdims=True))
           - jnp.exp(jnp.sum(lp[2:3] * lp[3:4], axis=-1, keepdims=True)) + lambda_init)
    on = acc_ref[...] / l_ref[...]
    o = (on[:, 0:TQ] - lam * on[:, TQ:]).T
    ms = jnp.mean(o * o, axis=-1, keepdims=True)
    y = o * lax.rsqrt(ms + RMS_EPS) * ng_ref[...] * (1.0 - lambda_init)
    o_ref[0] = y.astype(o_ref.dtype)


def _attn(z_diff, lam_p, norm_g, lambda_init):
    H = DIFF_HEADS
    return pl.pallas_call(
        functools.partial(_attn_kernel, lambda_init=lambda_init),
        grid=(BATCH, H, LP // TQ),
        in_specs=[
            pl.BlockSpec((1, TQ, 2 * DIFF_HEAD_DIM), lambda b, h, i: (b, i, h)),
            pl.BlockSpec((1, LP, 2 * DIFF_HEAD_DIM), lambda b, h, i: (b, 0, H + h)),
            pl.BlockSpec((1, LP, 2 * DIFF_HEAD_DIM), lambda b, h, i: (b, 0, 2 * H + h)),
            pl.BlockSpec((4, DIFF_HEAD_DIM), lambda b, h, i: (0, 0)),
            pl.BlockSpec((1, 2 * DIFF_HEAD_DIM), lambda b, h, i: (0, 0)),
        ],
        out_specs=pl.BlockSpec((1, TQ, 2 * DIFF_HEAD_DIM), lambda b, h, i: (b, i, h)),
        out_shape=jax.ShapeDtypeStruct((BATCH, LP, DIFF_WIDTH), BF16),
        scratch_shapes=[
            pltpu.VMEM((TQ, 2 * TQ), F32),
            pltpu.VMEM((TQ, 2 * TQ), F32),
            pltpu.VMEM((1, 2 * TQ), F32),
            pltpu.VMEM((1, 2 * TQ), F32),
            pltpu.VMEM((2 * DIFF_HEAD_DIM, 2 * TQ), F32),
        ],
        compiler_params=pltpu.CompilerParams(
            dimension_semantics=("parallel", "parallel", "arbitrary"),
            vmem_limit_bytes=VMEM_LIMIT),
        name="diff_attn",
    )(z_diff, z_diff, z_diff, lam_p, norm_g)


def _merge_kernel(h_ref, hb_ref, yp_ref, yh_ref, yd_ref, wgp_ref, wgh_ref, wgd_ref,
                  wp_ref, wh_ref, wd_ref, wo_ref, g_ref, b_ref, o_ref, acc_ref):
    j = pl.program_id(1)

    @pl.when(j == 0)
    def _():
        acc_ref[...] = jnp.zeros_like(acc_ref)

    hb = hb_ref[...]
    merged = (_sigmoid(_dot(hb, wgp_ref[...])) * _dot(yp_ref[...], wp_ref[...])
              + _sigmoid(_dot(hb, wgh_ref[...])) * _dot(yh_ref[...], wh_ref[...])
              + _sigmoid(_dot(hb, wgd_ref[...])) * _dot(yd_ref[...], wd_ref[...]))
    acc_ref[...] += _dot(merged.astype(BF16), wo_ref[...])

    @pl.when(j == pl.num_programs(1) - 1)
    def _():
        y = DEEPNORM_ALPHA * h_ref[...] + acc_ref[...]
        o_ref[...] = _layer_norm(y, g_ref[...], b_ref[...])


def _merge(h, hb, y_pool, y_hgrn, y_diff, w_gates, wp, wh, wd, wo, g, b):
    nj = D_MODEL // TN_MERGE
    row = lambda w: pl.BlockSpec((TM, w), lambda i, j: (i, 0))
    gate = lambda k: pl.BlockSpec((D_MODEL, TN_MERGE), lambda i, j, k=k: (0, k * nj + j))
    colw = lambda w: pl.BlockSpec((w, TN_MERGE), lambda i, j: (0, j))
    vec = pl.BlockSpec((1, D_MODEL), lambda i, j: (0, 0))
    return pl.pallas_call(
        _merge_kernel,
        grid=(NTOK // TM, nj),
        in_specs=[
            row(D_MODEL), row(D_MODEL), row(POOL_WIDTH), row(HGRN_WIDTH), row(DIFF_WIDTH),
            gate(0), gate(1), gate(2),
            colw(POOL_WIDTH), colw(HGRN_WIDTH), colw(DIFF_WIDTH),
            pl.BlockSpec((TN_MERGE, D_MODEL), lambda i, j: (j, 0)),
            vec, vec,
        ],
        out_specs=pl.BlockSpec((TM, D_MODEL), lambda i, j: (i, 0)),
        out_shape=jax.ShapeDtypeStruct((NTOK, D_MODEL), F32),
        scratch_shapes=[pltpu.VMEM((TM, D_MODEL), F32)],
        compiler_params=pltpu.CompilerParams(
            dimension_semantics=("parallel", "arbitrary"), vmem_limit_bytes=VMEM_LIMIT),
        name="merge",
    )(h, hb, y_pool, y_hgrn, y_diff, w_gates, w_gates, w_gates, wp, wh, wd, wo, g, b)


def kernel(x, meta_tokens, ln_g, ln_b, ffn_w_in, ffn_w_out, w_in, pool_w, pool_scale,
           hgrn_lower_bounds, hgrn_norm_g, diff_lambda, diff_norm_g,
           w_branch_pool, w_branch_hgrn, w_branch_diff, w_out):
    assert x.shape == (BATCH, SEQ, D_MODEL)
    meta = jnp.broadcast_to(meta_tokens[None].astype(x.dtype), (BATCH, N_META, D_MODEL))
    pad = jnp.zeros((BATCH, LP - SEQ_TOTAL, D_MODEL), x.dtype)
    h = jnp.concatenate([meta, x, pad], axis=1).reshape(NTOK, D_MODEL)

    for l in range(DEPTH):
        vec = lambda a, k: a[l, k][None, :]
        h, hb = _ffn(h, ffn_w_in[l, 0].astype(BF16), ffn_w_out[l, 0].astype(BF16),
                     vec(ln_g, 0), vec(ln_b, 0))

        w_pool_b = w_in[l][:, :O_POOL].astype(BF16)
        w_mix_b = w_in[l][:, O_POOL:O_DIFF].astype(BF16)
        w_gates_b = w_in[l][:, O_DIFF:].astype(BF16)
        z_pool = _proj(hb, w_pool_b, 0, POOL_WIDTH, POOL_WIDTH, F32, "proj_pool")
        z_hgrn = _proj(hb, w_mix_b, 0, 4 * HGRN_WIDTH, HGRN_WIDTH, F32, "proj_hgrn")
        z_diff = _proj(hb, w_mix_b, 4 * HGRN_WIDTH, 3 * DIFF_WIDTH, DIFF_WIDTH, BF16, "proj_diff")

        y_pool = _pool(z_pool.reshape(BATCH, LP, POOL_WIDTH), pool_w[l].astype(BF16),
                       pool_scale[l][None, :])
        y_hgrn = _hgrn(z_hgrn.reshape(BATCH, LP, 4 * HGRN_WIDTH), hgrn_lower_bounds,
                       hgrn_norm_g[l][None, :], l)
        lambda_init = 0.8 - 0.6 * math.exp(-0.3 * l)
        y_diff = _attn(z_diff.reshape(BATCH, LP, 3 * DIFF_WIDTH), diff_lambda[l],
                       diff_norm_g[l][None, :], lambda_init)

        h = _merge(h, hb, y_pool.reshape(NTOK, POOL_WIDTH), y_hgrn.reshape(NTOK, HGRN_WIDTH),
                   y_diff.reshape(NTOK, DIFF_WIDTH), w_gates_b,
                   w_branch_pool[l].astype(BF16), w_branch_hgrn[l].astype(BF16),
                   w_branch_diff[l].astype(BF16), w_out[l].astype(BF16),
                   vec(ln_g, 1), vec(ln_b, 1))

        h, _ = _ffn(h, ffn_w_in[l, 1].astype(BF16), ffn_w_out[l, 1].astype(BF16),
                    vec(ln_g, 2), vec(ln_b, 2))

    return h.reshape(BATCH, LP, D_MODEL)[:, N_META:SEQ_TOTAL]
```

```python
import functools
import math

import numpy as np
import jax
import jax.numpy as jnp
from jax import lax
from jax.experimental import pallas as pl
from jax.experimental.pallas import tpu as pltpu

F32 = jnp.float32
BF16 = jnp.bfloat16

D_MODEL = 2048
BATCH = 2
SEQ = 4096
DEPTH = 2
N_META = 16
SEQ_TOTAL = N_META + SEQ
LANES = 128
LP = -(-SEQ_TOTAL // LANES) * LANES
NTOK = BATCH * LP

POOL_WINDOWS = (2, 4, 8, 16)
POOL_GROUP_DIM = 128
POOL_WIDTH = 512
POOL_HALO = 16
HGRN_WIDTH = 768
HGRN_HEAD_DIM = 128
HGRN_HEADS = 6
HGRN_CHUNK = 128
DIFF_WIDTH = 768
DIFF_HEAD_DIM = 64
DIFF_HEADS = 6
D_FF = 5632
DEEPNORM_ALPHA = (2.0 * DEPTH) ** 0.25
LN_EPS = 1e-5
RMS_EPS = 1e-6

O_POOL = POOL_WIDTH
O_HGRN = O_POOL + 4 * HGRN_WIDTH
O_DIFF = O_HGRN + 3 * DIFF_WIDTH

TM = 528
TM_FFN = 528
TM_PROJ = 2112
TM_MIX = 1056
TF = 512
TN_MIX = 256
TQ = 384
TP = 384
VMEM_LIMIT = 56 * 1024 * 1024


def _sigmoid(x):
    return 1.0 / (1.0 + jnp.exp(-x))


def _layer_norm(y, g, b):
    mu = jnp.mean(y, axis=-1, keepdims=True)
    yc = y - mu
    var = jnp.mean(yc * yc, axis=-1, keepdims=True)
    return yc * lax.rsqrt(var + LN_EPS) * g + b


def _dot(a, b):
    return jnp.dot(a, b, preferred_element_type=F32)


def _dot_nt(a, b):
    return lax.dot_general(a, b, (((1,), (1,)), ((), ())), preferred_element_type=F32)


def _dot_tn(a, b):
    return lax.dot_general(a, b, (((0,), (0,)), ((), ())), preferred_element_type=F32)


def _ffn_kernel(h_ref, wa_ref, wu_ref, wo_ref, g_ref, b_ref, o_ref, ob_ref, hb_ref, acc_ref, *, ln):
    j = pl.program_id(1)

    @pl.when(j == 0)
    def _():
        hb_ref[...] = h_ref[...].astype(BF16)
        acc_ref[...] = jnp.zeros_like(acc_ref)

    hb = hb_ref[...]
    a = _dot(hb, wa_ref[...])
    u = _dot(hb, wu_ref[...])
    act = (a * _sigmoid(a)) * u
    acc_ref[...] += _dot(act.astype(BF16), wo_ref[...])

    @pl.when(j == pl.num_programs(1) - 1)
    def _():
        y = DEEPNORM_ALPHA * h_ref[...] + 0.5 * acc_ref[...]
        out = _layer_norm(y, g_ref[ln:ln + 1, :], b_ref[ln:ln + 1, :])
        o_ref[...] = out
        ob_ref[...] = out.astype(BF16)


def _ffn(h, w_in_b, w_out_b, ln_g, ln_b, layer, which):
    nff = D_FF // TF
    lnv = pl.BlockSpec((None, 3, D_MODEL), lambda i, j: (layer, 0, 0))
    return pl.pallas_call(
        functools.partial(_ffn_kernel, ln=2 * which),
        grid=(NTOK // TM_FFN, nff),
        in_specs=[
            pl.BlockSpec((TM_FFN, D_MODEL), lambda i, j: (i, 0)),
            pl.BlockSpec((None, None, D_MODEL, TF), lambda i, j: (layer, which, 0, j)),
            pl.BlockSpec((None, None, D_MODEL, TF), lambda i, j: (layer, which, 0, j + nff)),
            pl.BlockSpec((None, None, TF, D_MODEL), lambda i, j: (layer, which, j, 0)),
            lnv, lnv,
        ],
        out_specs=[
            pl.BlockSpec((TM_FFN, D_MODEL), lambda i, j: (i, 0)),
            pl.BlockSpec((TM_FFN, D_MODEL), lambda i, j: (i, 0)),
        ],
        out_shape=[
            jax.ShapeDtypeStruct((NTOK, D_MODEL), F32),
            jax.ShapeDtypeStruct((NTOK, D_MODEL), BF16),
        ],
        scratch_shapes=[
            pltpu.VMEM((TM_FFN, D_MODEL), BF16),
            pltpu.VMEM((TM_FFN, D_MODEL), F32),
        ],
        compiler_params=pltpu.CompilerParams(
            dimension_semantics=("parallel", "arbitrary"), vmem_limit_bytes=VMEM_LIMIT),
        name="ffn",
    )(h, w_in_b, w_in_b, w_out_b, ln_g, ln_b)


def _proj_kernel(x_ref, w_ref, o_ref):
    o_ref[...] = _dot(x_ref[...], w_ref[...]).astype(o_ref.dtype)


def _proj(xb, w_in_b, layer, col0, ncols, tm, tn, out_dtype, name):
    assert col0 % tn == 0 and ncols % tn == 0 and NTOK % tm == 0
    first = col0 // tn
    return pl.pallas_call(
        _proj_kernel,
        grid=(NTOK // tm, ncols // tn),
        in_specs=[
            pl.BlockSpec((tm, D_MODEL), lambda i, j: (i, 0)),
            pl.BlockSpec((None, D_MODEL, tn), lambda i, j: (layer, 0, first + j)),
        ],
        out_specs=pl.BlockSpec((tm, tn), lambda i, j: (i, j)),
        out_shape=jax.ShapeDtypeStruct((NTOK, ncols), out_dtype),
        compiler_params=pltpu.CompilerParams(
            dimension_semantics=("parallel", "arbitrary"), vmem_limit_bytes=VMEM_LIMIT),
        name=name,
    )(xb, w_in_b)


def _pool_kernel(z_ref, w_ref, s_ref, o_ref, ext_ref, *, layer):
    i = pl.program_id(1)
    u = z_ref[0]

    @pl.when(i == 0)
    def _():
        ext_ref[0:POOL_HALO, :] = jnp.zeros((POOL_HALO, POOL_WIDTH), F32)

    ext_ref[POOL_HALO:, :] = u
    pos = i * TP + lax.broadcasted_iota(jnp.int32, (TP, 1), 0)
    outs = []
    for gi, w in enumerate(POOL_WINDOWS):
        sl = slice(gi * POOL_GROUP_DIM, (gi + 1) * POOL_GROUP_DIM)
        win = ext_ref[POOL_HALO:POOL_HALO + TP, sl]
        for d in range(1, w):
            win = win + ext_ref[POOL_HALO - d:POOL_HALO - d + TP, sl]
        cnt = jnp.minimum(pos + 1, w).astype(F32)
        p = win / cnt - u[:, sl]
        y = _dot(p.astype(BF16), w_ref[gi])
        outs.append(y * s_ref[layer:layer + 1, sl])
    o_ref[0] = jnp.concatenate(outs, axis=-1).astype(o_ref.dtype)
    ext_ref[0:POOL_HALO, :] = u[TP - POOL_HALO:, :]


def _pool(z_pool, pool_w_b, pool_scale, layer):
    return pl.pallas_call(
        functools.partial(_pool_kernel, layer=layer),
        grid=(BATCH, LP // TP),
        in_specs=[
            pl.BlockSpec((1, TP, POOL_WIDTH), lambda b, i: (b, i, 0)),
            pl.BlockSpec((None, 4, POOL_GROUP_DIM, POOL_GROUP_DIM), lambda b, i: (layer, 0, 0, 0)),
            pl.BlockSpec((DEPTH, POOL_WIDTH), lambda b, i: (0, 0)),
        ],
        out_specs=pl.BlockSpec((1, TP, POOL_WIDTH), lambda b, i: (b, i, 0)),
        out_shape=jax.ShapeDtypeStruct((BATCH, LP, POOL_WIDTH), BF16),
        scratch_shapes=[pltpu.VMEM((POOL_HALO + TP, POOL_WIDTH), F32)],
        compiler_params=pltpu.CompilerParams(
            dimension_semantics=("parallel", "arbitrary"), vmem_limit_bytes=VMEM_LIMIT),
        name="pool",
    )(z_pool, pool_w_b, pool_scale)


def _pair_levels():
    t = np.arange(HGRN_CHUNK)[:, None]
    s = np.arange(HGRN_CHUNK)[None, :]
    x = t ^ s
    lvl = np.where(x > 0, 1 << np.floor(np.log2(np.maximum(x, 1))).astype(np.int64), 0)
    return np.where(t >= s, lvl, -1).astype(np.int32)


def _hgrn_kernel(q_ref, f_ref, v_ref, g_ref, lb_ref, ng_ref, lvl_ref, o_ref, st_ref, *, layer):
    c = pl.program_id(1)
    T, K = HGRN_CHUNK, HGRN_HEAD_DIM

    @pl.when(c == 0)
    def _():
        st_ref[...] = jnp.zeros_like(st_ref)

    lbp = lb_ref[...]
    e = jnp.exp(lbp - jnp.max(lbp, axis=0, keepdims=True))
    sm = e / jnp.sum(e, axis=0, keepdims=True)
    lb_all = jnp.sum(sm[0:layer + 1], axis=0, keepdims=True) - sm[0:1]

    lvl = lvl_ref[...]
    row = lax.broadcasted_iota(jnp.int32, (T, K), 0)
    ng = ng_ref[layer:layer + 1, :]

    for h in range(HGRN_HEADS):
        hs = slice(h * K, (h + 1) * K)
        lb = lb_all[:, hs]
        f = lb + (1.0 - lb) * _sigmoid(f_ref[0, :, hs])
        logf = jnp.log(f)
        kk = 1.0 - f
        q = q_ref[0, :, hs] * (K ** -0.5)
        v = v_ref[0, :, hs].astype(BF16)

        b = logf
        for sh in (1, 2, 4, 8, 16, 32, 64):
            b = b + jnp.where(row >= sh, pltpu.roll(b, sh, axis=0), 0.0)

        st = st_ref[h]
        o = _dot_nt((q * jnp.exp(b)).astype(BF16), st.astype(BF16))

        att = jnp.where(lvl == 0, _dot_nt(q.astype(BF16), kk.astype(BF16)), 0.0)
        e_end = b
        p_beg = jnp.where(row >= 1, pltpu.roll(b, 1, axis=0), 0.0)
        for w in (1, 2, 4, 8, 16, 32, 64):
            lower = (row & w) == 0
            arg = jnp.minimum(jnp.where(lower, e_end - b, b - p_beg), 0.0)
            x = (jnp.where(lower, kk, q) * jnp.exp(arg)).astype(BF16)
            att = jnp.where(lvl == w, _dot_nt(x, x), att)
            if w < T // 2:
                e_end = jnp.where(lower, pltpu.roll(e_end, T - w, axis=0), e_end)
                p_beg = jnp.where(lower, p_beg, pltpu.roll(p_beg, w, axis=0))
        o = o + _dot(att.astype(BF16), v)

        b_last = b[T - 1:T, :]
        khat = (kk * jnp.exp(b_last - b)).astype(BF16)
        st_ref[h] = jnp.exp(b_last) * st + _dot_tn(v, khat)

        ms = jnp.mean(o * o, axis=-1, keepdims=True)
        og = g_ref[0, :, hs]
        y = o * lax.rsqrt(ms + RMS_EPS) * ng * (og * _sigmoid(og))
        o_ref[0, :, hs] = y.astype(o_ref.dtype)


def _hgrn(z_hgrn, lower_bounds, norm_g, layer):
    nchunk = LP // HGRN_CHUNK
    blk = lambda k: pl.BlockSpec((1, HGRN_CHUNK, HGRN_WIDTH), lambda b, c, k=k: (b, c, k))
    return pl.pallas_call(
        functools.partial(_hgrn_kernel, layer=layer),
        grid=(BATCH, nchunk),
        in_specs=[
            blk(0), blk(1), blk(2), blk(3),
            pl.BlockSpec((DEPTH, HGRN_WIDTH), lambda b, c: (0, 0)),
            pl.BlockSpec((DEPTH, HGRN_HEAD_DIM), lambda b, c: (0, 0)),
            pl.BlockSpec((HGRN_CHUNK, HGRN_CHUNK), lambda b, c: (0, 0)),
        ],
        out_specs=pl.BlockSpec((1, HGRN_CHUNK, HGRN_WIDTH), lambda b, c: (b, c, 0)),
        out_shape=jax.ShapeDtypeStruct((BATCH, LP, HGRN_WIDTH), BF16),
        scratch_shapes=[pltpu.VMEM((HGRN_HEADS, HGRN_HEAD_DIM, HGRN_HEAD_DIM), F32)],
        compiler_params=pltpu.CompilerParams(
            dimension_semantics=("parallel", "arbitrary"), vmem_limit_bytes=VMEM_LIMIT),
        name="hgrn",
    )(z_hgrn, z_hgrn, z_hgrn, z_hgrn, lower_bounds, norm_g, jnp.asarray(_pair_levels()))


def _attn_kernel(q_ref, k_ref, v_ref, lam_ref, ng_ref, o_ref,
                 sa_ref, sb_ref, m_ref, l_ref, acc_ref, *, lambda_init, layer):
    i = pl.program_id(2)
    d = DIFF_HEAD_DIM
    q = q_ref[0] * (d ** -0.5)
    lane = lax.broadcasted_iota(jnp.int32, (TQ, 2 * d), 1)
    zero = jnp.zeros_like(q)
    q2 = jnp.concatenate([jnp.where(lane < d, q, zero), jnp.where(lane >= d, q, zero)], axis=0)

    def scores(j, dst_ref):
        off = pl.multiple_of(j * TQ, TQ)
        dst_ref[...] = _dot_nt(k_ref[0, pl.ds(off, TQ), :], q2)

    def update(j, src_ref, masked):
        off = pl.multiple_of(j * TQ, TQ)
        v = v_ref[0, pl.ds(off, TQ), :]
        s = src_ref[...]
        if masked:
            kr = lax.broadcasted_iota(jnp.int32, (TQ, 2 * TQ), 0)
            qc = lax.broadcasted_iota(jnp.int32, (TQ, 2 * TQ), 1)
            qc = jnp.where(qc >= TQ, qc - TQ, qc)
            s = jnp.where(kr <= qc, s, -1e30)
        m = m_ref[...]
        m_new = jnp.maximum(m, jnp.max(s, axis=0, keepdims=True))
        p = jnp.exp(s - m_new)
        alpha = jnp.exp(m - m_new)
        m_ref[...] = m_new
        l_ref[...] = alpha * l_ref[...] + jnp.sum(p, axis=0, keepdims=True)
        acc_ref[...] = alpha * acc_ref[...] + _dot_tn(v, p.astype(BF16))

    m_ref[...] = jnp.full(m_ref.shape, -1e30, F32)
    l_ref[...] = jnp.zeros(l_ref.shape, F32)
    acc_ref[...] = jnp.zeros(acc_ref.shape, F32)

    scores(0, sa_ref)
    scores(i, sb_ref)
    update(i, sb_ref, masked=True)

    @pl.when(i % 2 == 1)
    def _():
        scores(i - 1, sb_ref)
        update(i - 1, sb_ref, masked=False)

    def pair(t, carry):
        a = 2 * t
        scores(a + 1, sb_ref)
        update(a, sa_ref, masked=False)
        scores(a + 2, sa_ref)
        update(a + 1, sb_ref, masked=False)
        return carry

    lax.fori_loop(0, i // 2, pair, 0)

    lp = lam_ref[...]
    lam = (jnp.exp(jnp.sum(lp[0:1] * lp[1:2], axis=-1, keepdims=True))
           - jnp.exp(jnp.sum(lp[2:3] * lp[3:4], axis=-1, keepdims=True)) + lambda_init)
    on = acc_ref[...] / l_ref[...]
    o = (on[:, 0:TQ] - lam * on[:, TQ:]).T
    ms = jnp.mean(o * o, axis=-1, keepdims=True)
    y = o * lax.rsqrt(ms + RMS_EPS) * ng_ref[layer:layer + 1, :] * (1.0 - lambda_init)
    o_ref[0] = y.astype(o_ref.dtype)


def _attn(z_diff, lam_p, norm_g, layer):
    H = DIFF_HEADS
    lambda_init = 0.8 - 0.6 * math.exp(-0.3 * layer)
    return pl.pallas_call(
        functools.partial(_attn_kernel, lambda_init=lambda_init, layer=layer),
        grid=(BATCH, H, LP // TQ),
        in_specs=[
            pl.BlockSpec((1, TQ, 2 * DIFF_HEAD_DIM), lambda b, h, i: (b, i, h)),
            pl.BlockSpec((1, LP, 2 * DIFF_HEAD_DIM), lambda b, h, i: (b, 0, H + h)),
            pl.BlockSpec((1, LP, 2 * DIFF_HEAD_DIM), lambda b, h, i: (b, 0, 2 * H + h)),
            pl.BlockSpec((None, 4, DIFF_HEAD_DIM), lambda b, h, i: (layer, 0, 0)),
            pl.BlockSpec((DEPTH, 2 * DIFF_HEAD_DIM), lambda b, h, i: (0, 0)),
        ],
        out_specs=pl.BlockSpec((1, TQ, 2 * DIFF_HEAD_DIM), lambda b, h, i: (b, i, h)),
        out_shape=jax.ShapeDtypeStruct((BATCH, LP, DIFF_WIDTH), BF16),
        scratch_shapes=[
            pltpu.VMEM((TQ, 2 * TQ), F32),
            pltpu.VMEM((TQ, 2 * TQ), F32),
            pltpu.VMEM((1, 2 * TQ), F32),
            pltpu.VMEM((1, 2 * TQ), F32),
            pltpu.VMEM((2 * DIFF_HEAD_DIM, 2 * TQ), F32),
        ],
        compiler_params=pltpu.CompilerParams(
            dimension_semantics=("parallel", "parallel", "arbitrary"),
            vmem_limit_bytes=VMEM_LIMIT),
        name="diff_attn",
    )(z_diff, z_diff, z_diff, lam_p, norm_g)


def _mix_kernel(hb_ref, yp_ref, yh_ref, yd_ref, wgp_ref, wgh_ref, wgd_ref,
                wp_ref, wh_ref, wd_ref, o_ref):
    hb = hb_ref[...]
    merged = (_sigmoid(_dot(hb, wgp_ref[...])) * _dot(yp_ref[...], wp_ref[...])
              + _sigmoid(_dot(hb, wgh_ref[...])) * _dot(yh_ref[...], wh_ref[...])
              + _sigmoid(_dot(hb, wgd_ref[...])) * _dot(yd_ref[...], wd_ref[...]))
    o_ref[...] = merged.astype(o_ref.dtype)


def _mix(hb, y_pool, y_hgrn, y_diff, w_in_b, wp, wh, wd, layer):
    assert O_DIFF % TN_MIX == 0 and D_MODEL % TN_MIX == 0
    nj = D_MODEL // TN_MIX
    g0 = O_DIFF // TN_MIX
    row = lambda w: pl.BlockSpec((TM_MIX, w), lambda i, j: (i, 0))
    gate = lambda k: pl.BlockSpec((None, D_MODEL, TN_MIX),
                                  lambda i, j, k=k: (layer, 0, g0 + k * nj + j))
    colw = lambda w: pl.BlockSpec((None, w, TN_MIX), lambda i, j: (layer, 0, j))
    return pl.pallas_call(
        _mix_kernel,
        grid=(NTOK // TM_MIX, nj),
        in_specs=[
            row(D_MODEL), row(POOL_WIDTH), row(HGRN_WIDTH), row(DIFF_WIDTH),
            gate(0), gate(1), gate(2),
            colw(POOL_WIDTH), colw(HGRN_WIDTH), colw(DIFF_WIDTH),
        ],
        out_specs=pl.BlockSpec((TM_MIX, TN_MIX), lambda i, j: (i, j)),
        out_shape=jax.ShapeDtypeStruct((NTOK, D_MODEL), BF16),
        compiler_params=pltpu.CompilerParams(
            dimension_semantics=("parallel", "arbitrary"), vmem_limit_bytes=VMEM_LIMIT),
        name="mix",
    )(hb, y_pool, y_hgrn, y_diff, w_in_b, w_in_b, w_in_b, wp, wh, wd)


def _out_kernel(h_ref, m_ref, wo_ref, g_ref, b_ref, o_ref):
    y = DEEPNORM_ALPHA * h_ref[...] + _dot(m_ref[...], wo_ref[...])
    o_ref[...] = _layer_norm(y, g_ref[1:2, :], b_ref[1:2, :])


def _out_proj(h, merged, wo, ln_g, ln_b, layer):
    vec = pl.BlockSpec((None, 3, D_MODEL), lambda i: (layer, 0, 0))
    return pl.pallas_call(
        _out_kernel,
        grid=(NTOK // TM,),
        in_specs=[
            pl.BlockSpec((TM, D_MODEL), lambda i: (i, 0)),
            pl.BlockSpec((TM, D_MODEL), lambda i: (i, 0)),
            pl.BlockSpec((None, D_MODEL, D_MODEL), lambda i: (layer, 0, 0)),
            vec, vec,
        ],
        out_specs=pl.BlockSpec((TM, D_MODEL), lambda i: (i, 0)),
        out_shape=jax.ShapeDtypeStruct((NTOK, D_MODEL), F32),
        compiler_params=pltpu.CompilerParams(
            dimension_semantics=("parallel",), vmem_limit_bytes=VMEM_LIMIT),
        name="out_proj",
    )(h, merged, wo, ln_g, ln_b)


def kernel(x, meta_tokens, ln_g, ln_b, ffn_w_in, ffn_w_out, w_in, pool_w, pool_scale,
           hgrn_lower_bounds, hgrn_norm_g, diff_lambda, diff_norm_g,
           w_branch_pool, w_branch_hgrn, w_branch_diff, w_out):
    assert x.shape == (BATCH, SEQ, D_MODEL)
    meta = jnp.broadcast_to(meta_tokens[None].astype(x.dtype), (BATCH, N_META, D_MODEL))
    pad = jnp.zeros((BATCH, LP - SEQ_TOTAL, D_MODEL), x.dtype)
    h = jnp.concatenate([meta, x, pad], axis=1).reshape(NTOK, D_MODEL)

    ffn_w_in_b, ffn_w_out_b, w_in_b = (a.astype(BF16) for a in (ffn_w_in, ffn_w_out, w_in))
    pool_w_b, wbp_b, wbh_b, wbd_b, w_out_b = (
        a.astype(BF16) for a in (pool_w, w_branch_pool, w_branch_hgrn, w_branch_diff, w_out))

    for l in range(DEPTH):
        h, hb = _ffn(h, ffn_w_in_b, ffn_w_out_b, ln_g, ln_b, l, 0)

        z_pool = _proj(hb, w_in_b, l, 0, POOL_WIDTH, TM_PROJ, 512, F32, "proj_pool")
        z_hgrn = _proj(hb, w_in_b, l, O_POOL, 4 * HGRN_WIDTH, TM_PROJ, 512, F32, "proj_hgrn")
        z_diff = _proj(hb, w_in_b, l, O_HGRN, 3 * DIFF_WIDTH, TM_PROJ, 256, BF16, "proj_diff")

        y_pool = _pool(z_pool.reshape(BATCH, LP, POOL_WIDTH), pool_w_b, pool_scale, l)
        y_hgrn = _hgrn(z_hgrn.reshape(BATCH, LP, 4 * HGRN_WIDTH), hgrn_lower_bounds,
                       hgrn_norm_g, l)
        y_diff = _attn(z_diff.reshape(BATCH, LP, 3 * DIFF_WIDTH), diff_lambda, diff_norm_g, l)

        merged = _mix(hb, y_pool.reshape(NTOK, POOL_WIDTH), y_hgrn.reshape(NTOK, HGRN_WIDTH),
                      y_diff.reshape(NTOK, DIFF_WIDTH), w_in_b, wbp_b, wbh_b, wbd_b, l)
        h = _out_proj(h, merged, w_out_b, ln_g, ln_b, l)

        h, _ = _ffn(h, ffn_w_in_b, ffn_w_out_b, ln_g, ln_b, l, 1)

    return h.reshape(BATCH, LP, D_MODEL)[:, N_META:SEQ_TOTAL]
```

```python
import functools
import math

import numpy as np
import jax
import jax.numpy as jnp
from jax import lax
from jax.experimental import pallas as pl
from jax.experimental.pallas import tpu as pltpu

F32 = jnp.float32
BF16 = jnp.bfloat16

D_MODEL = 2048
BATCH = 2
SEQ = 4096
DEPTH = 2
N_META = 16
SEQ_TOTAL = N_META + SEQ
LANES = 128
LP = -(-SEQ_TOTAL // LANES) * LANES
NTOK = BATCH * LP

POOL_WINDOWS = (2, 4, 8, 16)
POOL_GROUP_DIM = 128
POOL_WIDTH = 512
POOL_HALO = 16
HGRN_WIDTH = 768
HGRN_HEAD_DIM = 128
HGRN_HEADS = 6
HGRN_CHUNK = 128
DIFF_WIDTH = 768
DIFF_HEAD_DIM = 64
DIFF_HEADS = 6
D_FF = 5632
DEEPNORM_ALPHA = (2.0 * DEPTH) ** 0.25
LN_EPS = 1e-5
RMS_EPS = 1e-6
LOG2E = math.log2(math.e)

O_POOL = POOL_WIDTH
O_HGRN = O_POOL + 4 * HGRN_WIDTH
O_DIFF = O_HGRN + 3 * DIFF_WIDTH

TM = 528
TM_FFN = 528
TM_PROJ = 2112
TM_MIX = 1056
TF = 512
TN_MIX = 256
TQ = 384
TP = 384
VMEM_LIMIT = 56 * 1024 * 1024


def _sigmoid(x):
    return 1.0 / (1.0 + jnp.exp(-x))


def _layer_norm(y, g, b):
    mu = jnp.mean(y, axis=-1, keepdims=True)
    yc = y - mu
    var = jnp.mean(yc * yc, axis=-1, keepdims=True)
    return yc * lax.rsqrt(var + LN_EPS) * g + b


def _dot(a, b):
    return jnp.dot(a, b, preferred_element_type=F32)


def _dot_nt(a, b):
    return lax.dot_general(a, b, (((1,), (1,)), ((), ())), preferred_element_type=F32)


def _dot_tn(a, b):
    return lax.dot_general(a, b, (((0,), (0,)), ((), ())), preferred_element_type=F32)


def _ffn_kernel(h_ref, wa_ref, wu_ref, wo_ref, g_ref, b_ref, o_ref, *rest, ln):
    ob_ref = rest[0] if len(rest) == 3 else None
    hb_ref, acc_ref = rest[-2:]
    j = pl.program_id(1)

    @pl.when(j == 0)
    def _():
        hb_ref[...] = h_ref[...].astype(BF16)
        acc_ref[...] = jnp.zeros_like(acc_ref)

    hb = hb_ref[...]
    a = _dot(hb, wa_ref[...])
    u = _dot(hb, wu_ref[...])
    act = (a * _sigmoid(a)) * u
    acc_ref[...] += _dot(act.astype(BF16), wo_ref[...])

    @pl.when(j == pl.num_programs(1) - 1)
    def _():
        y = DEEPNORM_ALPHA * h_ref[...] + 0.5 * acc_ref[...]
        out = _layer_norm(y, g_ref[ln:ln + 1, :], b_ref[ln:ln + 1, :])
        o_ref[...] = out
        if ob_ref is not None:
            ob_ref[...] = out.astype(BF16)


def _ffn(h, w_in_b, w_out_b, ln_g, ln_b, layer, which, with_bf16_copy):
    nff = D_FF // TF
    lnv = pl.BlockSpec((None, 3, D_MODEL), lambda i, j: (layer, 0, 0))
    out_dtypes = (F32, BF16) if with_bf16_copy else (F32,)
    return pl.pallas_call(
        functools.partial(_ffn_kernel, ln=2 * which),
        grid=(NTOK // TM_FFN, nff),
        in_specs=[
            pl.BlockSpec((TM_FFN, D_MODEL), lambda i, j: (i, 0)),
            pl.BlockSpec((None, None, D_MODEL, TF), lambda i, j: (layer, which, 0, j)),
            pl.BlockSpec((None, None, D_MODEL, TF), lambda i, j: (layer, which, 0, j + nff)),
            pl.BlockSpec((None, None, TF, D_MODEL), lambda i, j: (layer, which, j, 0)),
            lnv, lnv,
        ],
        out_specs=[pl.BlockSpec((TM_FFN, D_MODEL), lambda i, j: (i, 0)) for _ in out_dtypes],
        out_shape=[jax.ShapeDtypeStruct((NTOK, D_MODEL), dt) for dt in out_dtypes],
        scratch_shapes=[
            pltpu.VMEM((TM_FFN, D_MODEL), BF16),
            pltpu.VMEM((TM_FFN, D_MODEL), F32),
        ],
        compiler_params=pltpu.CompilerParams(
            dimension_semantics=("parallel", "arbitrary"), vmem_limit_bytes=VMEM_LIMIT),
        name="ffn",
    )(h, w_in_b, w_in_b, w_out_b, ln_g, ln_b)


def _proj_kernel(x_ref, w_ref, o_ref, *, scaled_tiles, scale):
    z = _dot(x_ref[...], w_ref[...])
    if scaled_tiles:
        z = z * jnp.where(pl.program_id(1) < scaled_tiles, scale, 1.0)
    o_ref[...] = z.astype(o_ref.dtype)


def _proj(xb, w_in_b, layer, col0, ncols, tm, tn, out_dtype, name, scaled_cols=0, scale=1.0):
    assert col0 % tn == 0 and ncols % tn == 0 and scaled_cols % tn == 0 and NTOK % tm == 0
    first = col0 // tn
    return pl.pallas_call(
        functools.partial(_proj_kernel, scaled_tiles=scaled_cols // tn, scale=scale),
        grid=(NTOK // tm, ncols // tn),
        in_specs=[
            pl.BlockSpec((tm, D_MODEL), lambda i, j: (i, 0)),
            pl.BlockSpec((None, D_MODEL, tn), lambda i, j: (layer, 0, first + j)),
        ],
        out_specs=pl.BlockSpec((tm, tn), lambda i, j: (i, j)),
        out_shape=jax.ShapeDtypeStruct((NTOK, ncols), out_dtype),
        compiler_params=pltpu.CompilerParams(
            dimension_semantics=("parallel", "arbitrary"), vmem_limit_bytes=VMEM_LIMIT),
        name=name,
    )(xb, w_in_b)


def _pool_kernel(z_ref, w_ref, s_ref, o_ref, ext_ref, *, layer):
    i = pl.program_id(1)
    u = z_ref[0]

    @pl.when(i == 0)
    def _():
        ext_ref[0:POOL_HALO, :] = jnp.zeros((POOL_HALO, POOL_WIDTH), F32)

    ext_ref[POOL_HALO:, :] = u
    pos = i * TP + lax.broadcasted_iota(jnp.int32, (TP, 1), 0)
    outs = []
    for gi, w in enumerate(POOL_WINDOWS):
        sl = slice(gi * POOL_GROUP_DIM, (gi + 1) * POOL_GROUP_DIM)
        win = ext_ref[POOL_HALO:POOL_HALO + TP, sl]
        for d in range(1, w):
            win = win + ext_ref[POOL_HALO - d:POOL_HALO - d + TP, sl]
        cnt = jnp.minimum(pos + 1, w).astype(F32)
        p = win / cnt - u[:, sl]
        y = _dot(p.astype(BF16), w_ref[gi])
        outs.append(y * s_ref[layer:layer + 1, sl])
    o_ref[0] = jnp.concatenate(outs, axis=-1).astype(o_ref.dtype)
    ext_ref[0:POOL_HALO, :] = u[TP - POOL_HALO:, :]


def _pool(z_pool, pool_w_b, pool_scale, layer):
    return pl.pallas_call(
        functools.partial(_pool_kernel, layer=layer),
        grid=(BATCH, LP // TP),
        in_specs=[
            pl.BlockSpec((1, TP, POOL_WIDTH), lambda b, i: (b, i, 0)),
            pl.BlockSpec((None, 4, POOL_GROUP_DIM, POOL_GROUP_DIM), lambda b, i: (layer, 0, 0, 0)),
            pl.BlockSpec((DEPTH, POOL_WIDTH), lambda b, i: (0, 0)),
        ],
        out_specs=pl.BlockSpec((1, TP, POOL_WIDTH), lambda b, i: (b, i, 0)),
        out_shape=jax.ShapeDtypeStruct((BATCH, LP, POOL_WIDTH), BF16),
        scratch_shapes=[pltpu.VMEM((POOL_HALO + TP, POOL_WIDTH), F32)],
        compiler_params=pltpu.CompilerParams(
            dimension_semantics=("parallel", "arbitrary"), vmem_limit_bytes=VMEM_LIMIT),
        name="pool",
    )(z_pool, pool_w_b, pool_scale)


def _pair_levels():
    t = np.arange(HGRN_CHUNK)[:, None]
    s = np.arange(HGRN_CHUNK)[None, :]
    x = t ^ s
    lvl = np.where(x > 0, 1 << np.floor(np.log2(np.maximum(x, 1))).astype(np.int64), 0)
    return np.where(t >= s, lvl, -1).astype(np.int32)


def _hgrn_kernel(q_ref, f_ref, v_ref, g_ref, lb_ref, ng_ref, lvl_ref, o_ref, st_ref, *, layer):
    c = pl.program_id(1)
    T, K = HGRN_CHUNK, HGRN_HEAD_DIM

    @pl.when(c == 0)
    def _():
        st_ref[...] = jnp.zeros_like(st_ref)

    lbp = lb_ref[...]
    e = jnp.exp(lbp - jnp.max(lbp, axis=0, keepdims=True))
    sm = e / jnp.sum(e, axis=0, keepdims=True)
    lb_all = jnp.sum(sm[0:layer + 1], axis=0, keepdims=True) - sm[0:1]

    lvl = lvl_ref[...]
    row = lax.broadcasted_iota(jnp.int32, (T, K), 0)
    ng = ng_ref[layer:layer + 1, :]

    for h in range(HGRN_HEADS):
        hs = slice(h * K, (h + 1) * K)
        lb = lb_all[:, hs]
        f = lb + (1.0 - lb) * _sigmoid(f_ref[0, :, hs])
        logf = jnp.log(f) * LOG2E
        kk = 1.0 - f
        q = q_ref[0, :, hs] * (K ** -0.5)
        v = v_ref[0, :, hs].astype(BF16)

        b = logf
        for sh in (1, 2, 4, 8, 16, 32, 64):
            b = b + jnp.where(row >= sh, pltpu.roll(b, sh, axis=0), 0.0)

        st = st_ref[h]
        o = _dot_nt((q * jnp.exp2(b)).astype(BF16), st.astype(BF16))

        att = jnp.where(lvl == 0, _dot_nt(q.astype(BF16), kk.astype(BF16)), 0.0)
        e_end = b
        p_beg = jnp.where(row >= 1, pltpu.roll(b, 1, axis=0), 0.0)
        for w in (1, 2, 4, 8, 16, 32, 64):
            lower = (row & w) == 0
            arg = jnp.minimum(jnp.where(lower, e_end - b, b - p_beg), 0.0)
            x = (jnp.where(lower, kk, q) * jnp.exp2(arg)).astype(BF16)
            att = jnp.where(lvl == w, _dot_nt(x, x), att)
            if w < T // 2:
                e_end = jnp.where(lower, pltpu.roll(e_end, T - w, axis=0), e_end)
                p_beg = jnp.where(lower, p_beg, pltpu.roll(p_beg, w, axis=0))
        o = o + _dot(att.astype(BF16), v)

        b_last = b[T - 1:T, :]
        khat = (kk * jnp.exp2(b_last - b)).astype(BF16)
        st_ref[h] = jnp.exp2(b_last) * st + _dot_tn(v, khat)

        ms = jnp.mean(o * o, axis=-1, keepdims=True)
        og = g_ref[0, :, hs]
        y = o * lax.rsqrt(ms + RMS_EPS) * ng * (og * _sigmoid(og))
        o_ref[0, :, hs] = y.astype(o_ref.dtype)


def _hgrn(z_hgrn, lower_bounds, norm_g, layer):
    nchunk = LP // HGRN_CHUNK
    blk = lambda k: pl.BlockSpec((1, HGRN_CHUNK, HGRN_WIDTH), lambda b, c, k=k: (b, c, k))
    return pl.pallas_call(
        functools.partial(_hgrn_kernel, layer=layer),
        grid=(BATCH, nchunk),
        in_specs=[
            blk(0), blk(1), blk(2), blk(3),
            pl.BlockSpec((DEPTH, HGRN_WIDTH), lambda b, c: (0, 0)),
            pl.BlockSpec((DEPTH, HGRN_HEAD_DIM), lambda b, c: (0, 0)),
            pl.BlockSpec((HGRN_CHUNK, HGRN_CHUNK), lambda b, c: (0, 0)),
        ],
        out_specs=pl.BlockSpec((1, HGRN_CHUNK, HGRN_WIDTH), lambda b, c: (b, c, 0)),
        out_shape=jax.ShapeDtypeStruct((BATCH, LP, HGRN_WIDTH), BF16),
        scratch_shapes=[pltpu.VMEM((HGRN_HEADS, HGRN_HEAD_DIM, HGRN_HEAD_DIM), F32)],
        compiler_params=pltpu.CompilerParams(
            dimension_semantics=("parallel", "arbitrary"), vmem_limit_bytes=VMEM_LIMIT),
        name="hgrn",
    )(z_hgrn, z_hgrn, z_hgrn, z_hgrn, lower_bounds, norm_g, jnp.asarray(_pair_levels()))


def _attn_kernel(q_ref, k_ref, v_ref, lam_ref, ng_ref, o_ref,
                 sa_ref, sb_ref, sc_ref, m_ref, l_ref, acc_ref, *, lambda_init, layer):
    i = pl.program_id(2)
    d = DIFF_HEAD_DIM
    q = q_ref[0]
    lane = lax.broadcasted_iota(jnp.int32, (TQ, 2 * d), 1)
    zero = jnp.zeros_like(q)
    q2 = jnp.concatenate([jnp.where(lane < d, q, zero), jnp.where(lane >= d, q, zero)], axis=0)

    def scores(j, dst_ref):
        off = pl.multiple_of(j * TQ, TQ)
        dst_ref[...] = _dot_nt(k_ref[0, pl.ds(off, TQ), :], q2)

    def update(j, src_ref, masked):
        off = pl.multiple_of(j * TQ, TQ)
        v = v_ref[0, pl.ds(off, TQ), :]
        s = src_ref[...]
        if masked:
            kr = lax.broadcasted_iota(jnp.int32, (TQ, 2 * TQ), 0)
            qc = lax.broadcasted_iota(jnp.int32, (TQ, 2 * TQ), 1)
            qc = jnp.where(qc >= TQ, qc - TQ, qc)
            s = jnp.where(kr <= qc, s, -1e30)
        m = m_ref[...]
        m_new = jnp.maximum(m, jnp.max(s, axis=0, keepdims=True))
        p = jnp.exp2(s - m_new)
        alpha = jnp.exp2(m - m_new)
        m_ref[...] = m_new
        l_ref[...] = alpha * l_ref[...] + jnp.sum(p, axis=0, keepdims=True)
        acc_ref[...] = alpha * acc_ref[...] + _dot_tn(v, p.astype(BF16))

    m_ref[...] = jnp.full(m_ref.shape, -1e30, F32)
    l_ref[...] = jnp.zeros(l_ref.shape, F32)
    acc_ref[...] = jnp.zeros(acc_ref.shape, F32)

    @pl.when(i % 2 == 0)
    def _():
        scores(0, sa_ref)
        scores(i, sb_ref)
        update(i, sb_ref, masked=True)

    @pl.when(i % 2 == 1)
    def _():
        scores(0, sa_ref)
        scores(i - 1, sb_ref)
        scores(i, sc_ref)
        update(i - 1, sb_ref, masked=False)
        update(i, sc_ref, masked=True)

    def pair(t, carry):
        a = 2 * t
        scores(a + 1, sb_ref)
        update(a, sa_ref, masked=False)
        scores(a + 2, sa_ref)
        update(a + 1, sb_ref, masked=False)
        return carry

    lax.fori_loop(0, i // 2, pair, 0)

    lp = lam_ref[...]
    lam = (jnp.exp(jnp.sum(lp[0:1] * lp[1:2], axis=-1, keepdims=True))
           - jnp.exp(jnp.sum(lp[2:3] * lp[3:4], axis=-1, keepdims=True)) + lambda_init)
    on = acc_ref[...] / l_ref[...]
    o = (on[:, 0:TQ] - lam * on[:, TQ:]).T
    ms = jnp.mean(o * o, axis=-1, keepdims=True)
    y = o * lax.rsqrt(ms + RMS_EPS) * ng_ref[layer:layer + 1, :] * (1.0 - lambda_init)
    o_ref[0] = y.astype(o_ref.dtype)


def _attn(z_diff, lam_p, norm_g, layer):
    H = DIFF_HEADS
    lambda_init = 0.8 - 0.6 * math.exp(-0.3 * layer)
    return pl.pallas_call(
        functools.partial(_attn_kernel, lambda_init=lambda_init, layer=layer),
        grid=(BATCH, H, LP // TQ),
        in_specs=[
            pl.BlockSpec((1, TQ, 2 * DIFF_HEAD_DIM), lambda b, h, i: (b, i, h)),
            pl.BlockSpec((1, LP, 2 * DIFF_HEAD_DIM), lambda b, h, i: (b, 0, H + h)),
            pl.BlockSpec((1, LP, 2 * DIFF_HEAD_DIM), lambda b, h, i: (b, 0, 2 * H + h)),
            pl.BlockSpec((None, 4, DIFF_HEAD_DIM), lambda b, h, i: (layer, 0, 0)),
            pl.BlockSpec((DEPTH, 2 * DIFF_HEAD_DIM), lambda b, h, i: (0, 0)),
        ],
        out_specs=pl.BlockSpec((1, TQ, 2 * DIFF_HEAD_DIM), lambda b, h, i: (b, i, h)),
        out_shape=jax.ShapeDtypeStruct((BATCH, LP, DIFF_WIDTH), BF16),
        scratch_shapes=[
            pltpu.VMEM((TQ, 2 * TQ), F32),
            pltpu.VMEM((TQ, 2 * TQ), F32),
            pltpu.VMEM((TQ, 2 * TQ), F32),
            pltpu.VMEM((1, 2 * TQ), F32),
            pltpu.VMEM((1, 2 * TQ), F32),
            pltpu.VMEM((2 * DIFF_HEAD_DIM, 2 * TQ), F32),
        ],
        compiler_params=pltpu.CompilerParams(
            dimension_semantics=("parallel", "parallel", "arbitrary"),
            vmem_limit_bytes=VMEM_LIMIT),
        name="diff_attn",
    )(z_diff, z_diff, z_diff, lam_p, norm_g)


def _mix_kernel(hb_ref, yp_ref, yh_ref, yd_ref, wgp_ref, wgh_ref, wgd_ref,
                wp_ref, wh_ref, wd_ref, o_ref):
    hb = hb_ref[...]
    merged = (_sigmoid(_dot(hb, wgp_ref[...])) * _dot(yp_ref[...], wp_ref[...])
              + _sigmoid(_dot(hb, wgh_ref[...])) * _dot(yh_ref[...], wh_ref[...])
              + _sigmoid(_dot(hb, wgd_ref[...])) * _dot(yd_ref[...], wd_ref[...]))
    o_ref[...] = merged.astype(o_ref.dtype)


def _mix(hb, y_pool, y_hgrn, y_diff, w_in_b, wp, wh, wd, layer):
    assert O_DIFF % TN_MIX == 0 and D_MODEL % TN_MIX == 0
    nj = D_MODEL // TN_MIX
    g0 = O_DIFF // TN_MIX
    row = lambda w: pl.BlockSpec((TM_MIX, w), lambda i, j: (i, 0))
    gate = lambda k: pl.BlockSpec((None, D_MODEL, TN_MIX),
                                  lambda i, j, k=k: (layer, 0, g0 + k * nj + j))
    colw = lambda w: pl.BlockSpec((None, w, TN_MIX), lambda i, j: (layer, 0, j))
    return pl.pallas_call(
        _mix_kernel,
        grid=(NTOK // TM_MIX, nj),
        in_specs=[
            row(D_MODEL), row(POOL_WIDTH), row(HGRN_WIDTH), row(DIFF_WIDTH),
            gate(0), gate(1), gate(2),
            colw(POOL_WIDTH), colw(HGRN_WIDTH), colw(DIFF_WIDTH),
        ],
        out_specs=pl.BlockSpec((TM_MIX, TN_MIX), lambda i, j: (i, j)),
        out_shape=jax.ShapeDtypeStruct((NTOK, D_MODEL), BF16),
        compiler_params=pltpu.CompilerParams(
            dimension_semantics=("parallel", "arbitrary"), vmem_limit_bytes=VMEM_LIMIT),
        name="mix",
    )(hb, y_pool, y_hgrn, y_diff, w_in_b, w_in_b, w_in_b, wp, wh, wd)


def _out_kernel(h_ref, m_ref, wo_ref, g_ref, b_ref, o_ref):
    y = DEEPNORM_ALPHA * h_ref[...] + _dot(m_ref[...], wo_ref[...])
    o_ref[...] = _layer_norm(y, g_ref[1:2, :], b_ref[1:2, :])


def _out_proj(h, merged, wo, ln_g, ln_b, layer):
    vec = pl.BlockSpec((None, 3, D_MODEL), lambda i: (layer, 0, 0))
    return pl.pallas_call(
        _out_kernel,
        grid=(NTOK // TM,),
        in_specs=[
            pl.BlockSpec((TM, D_MODEL), lambda i: (i, 0)),
            pl.BlockSpec((TM, D_MODEL), lambda i: (i, 0)),
            pl.BlockSpec((None, D_MODEL, D_MODEL), lambda i: (layer, 0, 0)),
            vec, vec,
        ],
        out_specs=pl.BlockSpec((TM, D_MODEL), lambda i: (i, 0)),
        out_shape=jax.ShapeDtypeStruct((NTOK, D_MODEL), F32),
        compiler_params=pltpu.CompilerParams(
            dimension_semantics=("parallel",), vmem_limit_bytes=VMEM_LIMIT),
        name="out_proj",
    )(h, merged, wo, ln_g, ln_b)


def kernel(x, meta_tokens, ln_g, ln_b, ffn_w_in, ffn_w_out, w_in, pool_w, pool_scale,
           hgrn_lower_bounds, hgrn_norm_g, diff_lambda, diff_norm_g,
           w_branch_pool, w_branch_hgrn, w_branch_diff, w_out):
    assert x.shape == (BATCH, SEQ, D_MODEL)
    meta = jnp.broadcast_to(meta_tokens[None].astype(x.dtype), (BATCH, N_META, D_MODEL))
    pad = jnp.zeros((BATCH, LP - SEQ_TOTAL, D_MODEL), x.dtype)
    h = jnp.concatenate([meta, x, pad], axis=1).reshape(NTOK, D_MODEL)

    ffn_w_in_b, ffn_w_out_b, w_in_b = (a.astype(BF16) for a in (ffn_w_in, ffn_w_out, w_in))
    pool_w_b, wbp_b, wbh_b, wbd_b, w_out_b = (
        a.astype(BF16) for a in (pool_w, w_branch_pool, w_branch_hgrn, w_branch_diff, w_out))

    for l in range(DEPTH):
        h, hb = _ffn(h, ffn_w_in_b, ffn_w_out_b, ln_g, ln_b, l, 0, True)

        z_pool = _proj(hb, w_in_b, l, 0, POOL_WIDTH, TM_PROJ, 512, F32, "proj_pool")
        z_hgrn = _proj(hb, w_in_b, l, O_POOL, 4 * HGRN_WIDTH, TM_PROJ, 512, F32, "proj_hgrn")
        z_diff = _proj(hb, w_in_b, l, O_HGRN, 3 * DIFF_WIDTH, TM_PROJ, 256, BF16, "proj_diff",
                       scaled_cols=DIFF_WIDTH, scale=DIFF_HEAD_DIM ** -0.5 * LOG2E)

        y_pool = _pool(z_pool.reshape(BATCH, LP, POOL_WIDTH), pool_w_b, pool_scale, l)
        y_hgrn = _hgrn(z_hgrn.reshape(BATCH, LP, 4 * HGRN_WIDTH), hgrn_lower_bounds,
                       hgrn_norm_g, l)
        y_diff = _attn(z_diff.reshape(BATCH, LP, 3 * DIFF_WIDTH), diff_lambda, diff_norm_g, l)

        merged = _mix(hb, y_pool.reshape(NTOK, POOL_WIDTH), y_hgrn.reshape(NTOK, HGRN_WIDTH),
                      y_diff.reshape(NTOK, DIFF_WIDTH), w_in_b, wbp_b, wbh_b, wbd_b, l)
        h = _out_proj(h, merged, w_out_b, ln_g, ln_b, l)

        (h,) = _ffn(h, ffn_w_in_b, ffn_w_out_b, ln_g, ln_b, l, 1, False)

    return h.reshape(BATCH, LP, D_MODEL)[:, N_META:SEQ_TOTAL]
```
